```python
import math
import jax
import jax.numpy as jnp
from jax import lax
import numpy as np

D_MODEL = 2048
BATCH = 4
SEQ = 2048
DEPTH = 2

CTX_LEN = 256
GRID_W = 64
D_MIX = D_MODEL
ATT_WIDTH = D_MIX // 2
POOL_WIDTH = D_MIX // 4
CONV_WIDTH = D_MIX - ATT_WIDTH - POOL_WIDTH
ATT_HEADS = 8
ATT_HD = ATT_WIDTH // (2 * ATT_HEADS)
ATT_VD = 2 * ATT_HD
POOL_WINDOWS = (2, 4, 8, 16)
POOL_GROUPS = len(POOL_WINDOWS)
POOL_GD = POOL_WIDTH // POOL_GROUPS
CONV_K = 31
Q_BLOCK = 128
ROPE_BASE = 10000.0
EPS = 1e-6
SPLITS = (ATT_WIDTH, 2 * ATT_WIDTH, 3 * ATT_WIDTH, 4 * ATT_WIDTH,
          4 * ATT_WIDTH + POOL_WIDTH, 4 * ATT_WIDTH + 2 * POOL_WIDTH,
          4 * ATT_WIDTH + 2 * POOL_WIDTH + CONV_WIDTH,
          4 * ATT_WIDTH + 2 * POOL_WIDTH + 2 * CONV_WIDTH)
N_IN = 4 * ATT_WIDTH + 2 * POOL_WIDTH + 3 * CONV_WIDTH

kernel_name = "hybrid_pool_diffattn_conformer_dit_block"


def _rmsnorm(x, g):
    xf = x.astype(jnp.float32)
    y = xf * lax.rsqrt(jnp.mean(xf * xf, axis=-1, keepdims=True) + EPS)
    return (y * g.astype(jnp.float32)).astype(x.dtype)


def _layernorm(x, g, b):
    xf = x.astype(jnp.float32)
    mu = jnp.mean(xf, axis=-1, keepdims=True)
    var = jnp.mean(jnp.square(xf - mu), axis=-1, keepdims=True)
    y = (xf - mu) * lax.rsqrt(var + EPS) * g.astype(jnp.float32) + b.astype(jnp.float32)
    return y.astype(x.dtype)


def _heads_qk(t):
    return t.reshape(t.shape[0], t.shape[1], ATT_HEADS, 2, ATT_HD)


def _heads_v(t):
    return t.reshape(t.shape[0], t.shape[1], ATT_HEADS, ATT_VD)


def _axial_rope(t, row, col):
    n_freq = ATT_HD // 4
    inv_freq = ROPE_BASE ** (-jnp.arange(n_freq, dtype=jnp.float32) / n_freq)

    def rot(u, pos):
        ang = pos.astype(jnp.float32)[:, None] * inv_freq[None, :]
        cos = jnp.concatenate([jnp.cos(ang)] * 2, axis=-1)[None, :, None, None, :]
        sin = jnp.concatenate([jnp.sin(ang)] * 2, axis=-1)[None, :, None, None, :]
        u1, u2 = jnp.split(u, 2, axis=-1)
        rh = jnp.concatenate([-u2, u1], axis=-1)
        return (u.astype(jnp.float32) * cos + rh.astype(jnp.float32) * sin).astype(u.dtype)

    t_row, t_col = jnp.split(t, 2, axis=-1)
    return jnp.concatenate([rot(t_row, row), rot(t_col, col)], axis=-1)


def _diff_attention(q, k, v, lam):
    B, Lq = q.shape[0], q.shape[1]
    nb = Lq // Q_BLOCK
    qb = jnp.moveaxis(q.reshape(B, nb, Q_BLOCK, ATT_HEADS, 2, ATT_HD), 1, 0)
    scale = ATT_HD ** -0.5

    def block(qq):
        s = jnp.einsum('bqhcd,bkhcd->bhcqk', qq, k).astype(jnp.float32) * scale
        p = jax.nn.softmax(s, axis=-1)
        a = p[:, :, 0] - lam * p[:, :, 1]
        return jnp.einsum('bhqk,bkhe->bqhe', a.astype(v.dtype), v)

    o = lax.map(block, qb)
    return jnp.moveaxis(o, 0, 1).reshape(B, Lq, ATT_HEADS, ATT_VD)


def _multiscale_pool(u, w_pool, pool_scale):
    B, L, _ = u.shape
    ug = u.reshape(B, L, POOL_GROUPS, POOL_GD)
    cs = jnp.concatenate([jnp.zeros((B, 1, POOL_GROUPS, POOL_GD), jnp.float32),
                          jnp.cumsum(ug.astype(jnp.float32), axis=1)], axis=1)
    t = jnp.arange(L, dtype=jnp.int32)[:, None]
    halfw = jnp.array(POOL_WINDOWS, dtype=jnp.int32)[None, :] // 2
    lo = jnp.clip(t - halfw, 0, L)
    hi = jnp.clip(t + halfw, 0, L)
    gi = jnp.arange(POOL_GROUPS, dtype=jnp.int32)[None, :]
    win_sum = cs[:, hi, gi] - cs[:, lo, gi]
    mean = win_sum / (hi - lo).astype(jnp.float32)[None, :, :, None]
    d = (mean - ug.astype(jnp.float32)).astype(u.dtype)
    y = jnp.einsum('blgc,gcd->blgd', d, w_pool) * pool_scale.reshape(POOL_GROUPS, POOL_GD)
    return y.reshape(B, L, POOL_WIDTH)


def _conformer_conv(a, b, w_dw, b_dw, ln_g, ln_b, w_pw2):
    u = a * jax.nn.sigmoid(b)
    y = lax.conv_general_dilated(u, w_dw[:, None, :].astype(u.dtype), window_strides=(1,),
                                 padding=[(CONV_K // 2, CONV_K // 2)],
                                 dimension_numbers=('NWC', 'WIO', 'NWC'),
                                 feature_group_count=CONV_WIDTH) + b_dw
    y = jax.nn.silu(_layernorm(y, ln_g, ln_b))
    return y @ w_pw2


def _mix(q, k, v, g_att, u_pool, g_pool, a_conv, b_conv, g_conv, lam, lam_init,
         subln_g, w_pool, pool_scale, w_dw, b_dw, ln_g, ln_b, w_pw2, w_out):
    B, L = q.shape[0], q.shape[1]
    o = _diff_attention(q, k, v, lam)
    y_att = (_rmsnorm(o, subln_g) * (1.0 - lam_init)).reshape(B, L, ATT_WIDTH)
    y_pool = _multiscale_pool(u_pool, w_pool, pool_scale)
    y_conv = _conformer_conv(a_conv, b_conv, w_dw, b_dw, ln_g, ln_b, w_pw2)
    y = jnp.concatenate([y_att * jax.nn.silu(g_att),
                         y_pool * jax.nn.silu(g_pool),
                         y_conv * jax.nn.silu(g_conv)], axis=-1)
    return y @ w_out


def setup_inputs(seed: int = 0) -> dict:
    key = jax.random.key(seed)
    ks = jax.random.split(key, 24)
    f32 = jnp.float32
    n = lambda k, s, sc: jax.random.normal(k, s, f32) * sc
    return {
        "x": n(ks[0], (BATCH, SEQ, D_MODEL), 1.0),
        "c": n(ks[1], (BATCH, D_MODEL), 1.0),
        "ctx": n(ks[2], (BATCH, CTX_LEN, D_MODEL), 1.0),
        "c_ctx": n(ks[3], (D_MODEL,), 1.0),
        "w_mod": n(ks[4], (DEPTH, D_MODEL, 3 * D_MODEL), 0.5 * D_MODEL ** -0.5),
        "b_mod": n(ks[5], (DEPTH, 3 * D_MODEL), 0.02),
        "norm_g": 1.0 + n(ks[6], (DEPTH, D_MODEL), 0.05),
        "w_in": n(ks[7], (DEPTH, D_MODEL, N_IN), D_MODEL ** -0.5),
        "lambda_q1": n(ks[8], (DEPTH, ATT_HD), 0.1),
        "lambda_k1": n(ks[9], (DEPTH, ATT_HD), 0.1),
        "lambda_q2": n(ks[10], (DEPTH, ATT_HD), 0.1),
        "lambda_k2": n(ks[11], (DEPTH, ATT_HD), 0.1),
        "subln_g": 1.0 + n(ks[12], (DEPTH, ATT_VD), 0.05),
        "w_pool": n(ks[13], (DEPTH, POOL_GROUPS, POOL_GD, POOL_GD), POOL_GD ** -0.5),
        "pool_scale": 1.0 + n(ks[14], (DEPTH, POOL_WIDTH), 0.05),
        "w_dw": n(ks[15], (DEPTH, CONV_K, CONV_WIDTH), CONV_K ** -0.5),
        "b_dw": n(ks[16], (DEPTH, CONV_WIDTH), 0.02),
        "conv_ln_g": 1.0 + n(ks[17], (DEPTH, CONV_WIDTH), 0.05),
        "conv_ln_b": n(ks[18], (DEPTH, CONV_WIDTH), 0.02),
        "w_pw2": n(ks[19], (DEPTH, CONV_WIDTH, CONV_WIDTH), CONV_WIDTH ** -0.5),
        "w_out": n(ks[20], (DEPTH, D_MIX, D_MODEL), D_MIX ** -0.5),
        "final_g": 1.0 + n(ks[21], (D_MODEL,), 0.05),
    }


def reference(x, c, ctx, c_ctx, w_mod, b_mod, norm_g, w_in, lambda_q1, lambda_k1, lambda_q2, lambda_k2,
              subln_g, w_pool, pool_scale, w_dw, b_dw, conv_ln_g, conv_ln_b, w_pw2, w_out, final_g):
    B, L, _ = x.shape
    rows = L // GRID_W
    row = jnp.repeat(jnp.arange(rows, dtype=jnp.int32), GRID_W)
    col = jnp.tile(jnp.arange(GRID_W, dtype=jnp.int32), rows)
    s_lat = jax.nn.silu(c)
    s_ctx = jax.nn.silu(c_ctx)
    for l in range(DEPTH):
        last = l == DEPTH - 1
        lam_init = 0.8 - 0.6 * math.exp(-0.3 * l)
        lam = (jnp.exp(jnp.sum(lambda_q1[l].astype(jnp.float32) * lambda_k1[l].astype(jnp.float32)))
               - jnp.exp(jnp.sum(lambda_q2[l].astype(jnp.float32) * lambda_k2[l].astype(jnp.float32)))
               + lam_init)
        shift, scale, gate = jnp.split((s_lat @ w_mod[l] + b_mod[l])[:, None, :], 3, axis=-1)
        shift_c, scale_c, gate_c = jnp.split(s_ctx @ w_mod[l] + b_mod[l], 3, axis=-1)
        hx = _rmsnorm(x, norm_g[l]) * (1.0 + scale) + shift
        hc = _rmsnorm(ctx, norm_g[l]) * (1.0 + scale_c) + shift_c

        q, k, v, g_att, u_pool, g_pool, a_conv, b_conv, g_conv = jnp.split(hx @ w_in[l], SPLITS, axis=-1)
        q = _axial_rope(_heads_qk(q), row, col)
        k = _axial_rope(_heads_qk(k), row, col)
        v = _heads_v(v)
        if last:
            k_c, v_c = jnp.split(hc @ w_in[l][:, ATT_WIDTH:3 * ATT_WIDTH], 2, axis=-1)
        else:
            (q_c, k_c, v_c, g_att_c, u_pool_c, g_pool_c,
             a_conv_c, b_conv_c, g_conv_c) = jnp.split(hc @ w_in[l], SPLITS, axis=-1)
        k_c = _heads_qk(k_c)
        v_c = _heads_v(v_c)

        lw = (subln_g[l], w_pool[l], pool_scale[l], w_dw[l], b_dw[l],
              conv_ln_g[l], conv_ln_b[l], w_pw2[l], w_out[l])
        y = _mix(q, jnp.concatenate([k_c, k], axis=1), jnp.concatenate([v_c, v], axis=1),
                 g_att, u_pool, g_pool, a_conv, b_conv, g_conv, lam, lam_init, *lw)
        if not last:
            y_c = _mix(_heads_qk(q_c), k_c, v_c, g_att_c, u_pool_c, g_pool_c,
                       a_conv_c, b_conv_c, g_conv_c, lam, lam_init, *lw)
            ctx = ctx + gate_c * y_c
        x = x + gate * y
    return _rmsnorm(x, final_g)
```

```python
import functools
import math

import jax
import jax.numpy as jnp
from jax import lax
from jax.experimental import pallas as pl
from jax.experimental.pallas import tpu as pltpu

ATT_HEADS = 8
ATT_HD = 64
ATT_VD = 2 * ATT_HD
POOL_WINDOWS = (2, 4, 8, 16)
POOL_GD = 128
CONV_K = 31
GRID_W = 64
ROPE_BASE = 10000.0
EPS = 1e-6

LANES = 128
KV_CHUNK = 256
Q_SUB = 128
HALO = 16
VMEM_LIMIT = 56 * 1024 * 1024

F32 = jnp.float32
BF16 = jnp.bfloat16


def _cparams(n_axes):
    return pltpu.CompilerParams(dimension_semantics=("arbitrary",) * n_axes,
                                vmem_limit_bytes=VMEM_LIMIT)


def _silu(v):
    return v * jax.nn.sigmoid(v)


def _mod_kernel(cc_ref, w_ref, b_ref, lq1_ref, lk1_ref, lq2_ref, lk2_ref, li_ref, out_ref, lam_ref):
    s = _silu(cc_ref[...])
    out_ref[0] = jnp.dot(s, w_ref[0], precision=lax.Precision.HIGHEST,
                         preferred_element_type=F32) + b_ref[0]
    a1 = jnp.sum(lq1_ref[0] * lk1_ref[0], axis=-1, keepdims=True)
    a2 = jnp.sum(lq2_ref[0] * lk2_ref[0], axis=-1, keepdims=True)
    lam = jnp.exp(a1) - jnp.exp(a2) + li_ref[0]
    lam_ref[0] = jnp.broadcast_to(lam, lam_ref.shape[1:])


def _modulation(cc, w_mod, b_mod, lq1, lk1, lq2, lk2, lam_init):
    depth, d, n = w_mod.shape
    tn = n // 4 if n % (4 * LANES) == 0 else n
    vec = lambda a: a.reshape(depth, 1, a.shape[-1])
    lspec = pl.BlockSpec((1, 1, ATT_HD), lambda l, j: (l, 0, 0))
    return pl.pallas_call(
        _mod_kernel,
        grid=(depth, n // tn),
        in_specs=[pl.BlockSpec((8, d), lambda l, j: (0, 0)),
                  pl.BlockSpec((1, d, tn), lambda l, j: (l, 0, j)),
                  pl.BlockSpec((1, 1, tn), lambda l, j: (l, 0, j)),
                  lspec, lspec, lspec, lspec,
                  pl.BlockSpec((1, 1, LANES), lambda l, j: (l, 0, 0))],
        out_specs=[pl.BlockSpec((1, 8, tn), lambda l, j: (l, 0, j)),
                   pl.BlockSpec((1, 8, LANES), lambda l, j: (l, 0, 0))],
        out_shape=[jax.ShapeDtypeStruct((depth, 8, n), F32),
                   jax.ShapeDtypeStruct((depth, 8, LANES), F32)],
        compiler_params=_cparams(2),
        name="modulation",
    )(cc, w_mod, vec(b_mod), vec(lq1), vec(lk1), vec(lq2), vec(lk2), lam_init)


def _inproj_kernel(*refs, d_model, rope, n_rope_tiles, n_q_tiles):
    if rope:
        x_ref, mod_ref, g_ref, w_ref, cos_ref, sa_ref, sb_ref, out_ref, h_ref = refs
    else:
        x_ref, mod_ref, g_ref, w_ref, out_ref, h_ref = refs
    j = pl.program_id(1)

    @pl.when(j == 0)
    def _():
        xf = x_ref[...]
        ms = jnp.mean(xf * xf, axis=-1, keepdims=True)
        m = mod_ref[0]
        shift = m[:, :d_model]
        scale = m[:, d_model:2 * d_model]
        h = xf * lax.rsqrt(ms + EPS) * g_ref[...] * (1.0 + scale) + shift
        h_ref[...] = h.astype(BF16)

    acc = jnp.dot(h_ref[...], w_ref[...], preferred_element_type=F32)
    qs = jnp.where(j < n_q_tiles, ATT_HD ** -0.5, 1.0).astype(F32)
    if not rope:
        out_ref[...] = (acc * qs).astype(out_ref.dtype)
        return

    @pl.when(j < n_rope_tiles)
    def _():
        cos = cos_ref[...] * qs
        sa = sa_ref[...] * qs
        sb = sb_ref[...] * qs
        for hh in range(acc.shape[1] // LANES):
            t = acc[:, hh * LANES:(hh + 1) * LANES]
            r = (t * cos + pltpu.roll(t, LANES - ATT_HD // 4, 1) * sa
                 + pltpu.roll(t, ATT_HD // 4, 1) * sb)
            out_ref[:, hh * LANES:(hh + 1) * LANES] = r.astype(out_ref.dtype)

    @pl.when(j >= n_rope_tiles)
    def _():
        out_ref[...] = acc.astype(out_ref.dtype)


def _inproj(x2d, mod3, g, w, tabs, *, tm, tn, tiles_per_mod, mod_row0, n0, n_tiles, att_width, name):
    m_rows, d = x2d.shape
    rope = tabs is not None
    tiles_per_seq = None
    in_specs = [pl.BlockSpec((tm, d), lambda i, j: (i, 0)),
                pl.BlockSpec((1, 1, mod3.shape[-1]), lambda i, j: (mod_row0 + i // tiles_per_mod, 0, 0)),
                pl.BlockSpec((1, d), lambda i, j: (0, 0)),
                pl.BlockSpec((d, tn), lambda i, j: (0, n0 + j))]
    args = [x2d, mod3, g, w]
    if rope:
        tiles_per_seq = tabs[0].shape[0] // tm
        tspec = pl.BlockSpec((tm, LANES), lambda i, j: (i % tiles_per_seq, 0))
        in_specs += [tspec, tspec, tspec]
        args += list(tabs)
    kern = functools.partial(_inproj_kernel, d_model=d, rope=rope,
                             n_rope_tiles=max(2 * att_width // tn - n0, 0),
                             n_q_tiles=max(att_width // tn - n0, 0))
    return pl.pallas_call(
        kern,
        grid=(m_rows // tm, n_tiles),
        in_specs=in_specs,
        out_specs=pl.BlockSpec((tm, tn), lambda i, j: (i, j)),
        out_shape=jax.ShapeDtypeStruct((m_rows, n_tiles * tn), BF16),
        scratch_shapes=[pltpu.VMEM((tm, d), BF16)],
        compiler_params=_cparams(2),
        name=name,
    )(*args)


def _attn_kernel(*refs, n_ctx_chunks, n_lat_chunks, n_sub, out_scale):
    if n_lat_chunks:
        q_ref, kc_ref, vc_ref, kl_ref, vl_ref, g_ref, lam_ref, sg_ref, out_ref, vt_ref = refs
    else:
        q_ref, kc_ref, vc_ref, g_ref, lam_ref, sg_ref, out_ref, vt_ref = refs
        kl_ref = vl_ref = None
    chunks = [(kc_ref, c) for c in range(n_ctx_chunks)] + [(kl_ref, c) for c in range(n_lat_chunks)]

    @pl.when(pl.program_id(2) == 0)
    def _():
        srcs = [(vc_ref, c) for c in range(n_ctx_chunks)] + [(vl_ref, c) for c in range(n_lat_chunks)]
        for n, (ref, c) in enumerate(srcs):
            blk = ref[c * KV_CHUNK:(c + 1) * KV_CHUNK, :].astype(F32)
            vt_ref[:, n * KV_CHUNK:(n + 1) * KV_CHUNK] = blk.T.astype(BF16)

    row = lax.broadcasted_iota(jnp.int32, (ATT_VD, Q_SUB), 0)
    lam = lam_ref[0][0:1, :]
    sg = sg_ref[...]

    def sub_body(s, carry):
        r0 = pl.multiple_of(s * Q_SUB, Q_SUB)
        qt = q_ref[pl.ds(r0, Q_SUB), :].astype(F32).T
        qbd = jnp.concatenate([jnp.where(row < ATT_HD, qt, 0.0),
                               jnp.where(row >= ATT_HD, qt, 0.0)], axis=1).astype(BF16)
        m = l = acc = None
        for n, (k_ref, c) in enumerate(chunks):
            kch = k_ref[c * KV_CHUNK:(c + 1) * KV_CHUNK, :]
            st = jnp.dot(kch, qbd, preferred_element_type=F32)
            cmax = jnp.max(st, axis=0, keepdims=True)
            vt = vt_ref[:, n * KV_CHUNK:(n + 1) * KV_CHUNK]
            if n == 0:
                m = cmax
                p = jnp.exp(st - m)
                l = jnp.sum(p, axis=0, keepdims=True)
                acc = jnp.dot(vt, p.astype(BF16), preferred_element_type=F32)
            else:
                m_new = jnp.maximum(m, cmax)
                alpha = jnp.exp(m - m_new)
                p = jnp.exp(st - m_new)
                l = alpha * l + jnp.sum(p, axis=0, keepdims=True)
                acc = alpha * acc + jnp.dot(vt, p.astype(BF16), preferred_element_type=F32)
                m = m_new
        o = acc * (1.0 / l)
        ot = o[:, :Q_SUB] - lam * o[:, Q_SUB:]
        oq = ot.T
        ms = jnp.mean(oq * oq, axis=-1, keepdims=True)
        y = oq * lax.rsqrt(ms + EPS) * sg * out_scale
        gate = g_ref[pl.ds(r0, Q_SUB), :].astype(F32)
        out_ref[pl.ds(r0, Q_SUB), :] = (y * _silu(gate)).astype(out_ref.dtype)
        return carry

    lax.fori_loop(0, n_sub, sub_body, 0)


def _attention(q_arr, kv_ctx, kv_lat, lam, subln_g, *, batch, lq, tq, q_col, g_col, lc, kc_col, vc_col,
               ll, kl_col, vl_col, out_scale, name):
    nq = lq // tq
    hb = ATT_HEADS
    in_specs = [pl.BlockSpec((tq, LANES), lambda b, h, i: (b * nq + i, q_col + h)),
                pl.BlockSpec((lc, LANES), lambda b, h, i: (b, kc_col + h)),
                pl.BlockSpec((lc, LANES), lambda b, h, i: (b, vc_col + h))]
    args = [q_arr, kv_ctx, kv_ctx]
    if kv_lat is not None:
        in_specs += [pl.BlockSpec((ll, LANES), lambda b, h, i: (b, kl_col + h)),
                     pl.BlockSpec((ll, LANES), lambda b, h, i: (b, vl_col + h))]
        args += [kv_lat, kv_lat]
    else:
        ll = 0
    in_specs += [pl.BlockSpec((tq, LANES), lambda b, h, i: (b * nq + i, g_col + h)),
                 pl.BlockSpec((1, 8, LANES), lambda b, h, i: (0, 0, 0)),
                 pl.BlockSpec((1, ATT_VD), lambda b, h, i: (0, 0))]
    args += [q_arr, lam, subln_g]
    kern = functools.partial(_attn_kernel, n_ctx_chunks=lc // KV_CHUNK, n_lat_chunks=ll // KV_CHUNK,
                             n_sub=tq // Q_SUB, out_scale=out_scale)
    return pl.pallas_call(
        kern,
        grid=(batch, hb, nq),
        in_specs=in_specs,
        out_specs=pl.BlockSpec((tq, LANES), lambda b, h, i: (b * nq + i, h)),
        out_shape=jax.ShapeDtypeStruct((batch * lq, hb * ATT_VD), BF16),
        scratch_shapes=[pltpu.VMEM((ATT_VD, lc + ll), BF16)],
        compiler_params=_cparams(3),
        name=name,
    )(*args)


def _poolconv_kernel(up_c, up_p, up_n, gp_ref, a_c, a_p, a_n, b_c, b_p, b_n, gc_ref,
                     wpool_ref, pscale_ref, wdw_ref, bdw_ref, lng_ref, lnb_ref, wpw_ref,
                     ypool_ref, yconv_ref, u_ref, p_ref, z_ref, *, rows, tiles_per_seq, seq_len, row_blk):
    iseq = pl.program_id(0) % tiles_per_seq
    has_prev = iseq > 0
    has_next = iseq < tiles_per_seq - 1

    def glu(a_ref, b_ref):
        return a_ref[...].astype(F32) * jax.nn.sigmoid(b_ref[...].astype(F32))

    u_ref[0:HALO, :] = jnp.where(has_prev, glu(a_p, b_p), 0.0)
    u_ref[HALO:HALO + rows, :] = glu(a_c, b_c)
    u_ref[HALO + rows:, :] = jnp.where(has_next, glu(a_n, b_n), 0.0)
    p_ref[0:HALO, :] = jnp.where(has_prev, up_p[...].astype(F32), 0.0)
    p_ref[HALO:HALO + rows, :] = up_c[...].astype(F32)
    p_ref[HALO + rows:, :] = jnp.where(has_next, up_n[...].astype(F32), 0.0)

    half = CONV_K // 2
    for rb in range(rows // row_blk):
        base = HALO + rb * row_blk - half
        acc = u_ref[base:base + row_blk, :] * wdw_ref[0:1, :]
        for j in range(1, CONV_K):
            acc = acc + u_ref[base + j:base + j + row_blk, :] * wdw_ref[j:j + 1, :]
        y = acc + bdw_ref[...]
        mu = jnp.mean(y, axis=-1, keepdims=True)
        yc = y - mu
        var = jnp.mean(yc * yc, axis=-1, keepdims=True)
        z = yc * lax.rsqrt(var + EPS) * lng_ref[...] + lnb_ref[...]
        z_ref[rb * row_blk:(rb + 1) * row_blk, :] = _silu(z).astype(BF16)
    yconv = jnp.dot(z_ref[...], wpw_ref[...], preferred_element_type=F32)
    yconv_ref[...] = (yconv * _silu(gc_ref[...].astype(F32))).astype(yconv_ref.dtype)

    t = iseq * rows + lax.broadcasted_iota(jnp.int32, (rows, POOL_GD), 0)
    for g, w in enumerate(POOL_WINDOWS):
        cols = slice(g * POOL_GD, (g + 1) * POOL_GD)
        hw = w // 2
        ssum = p_ref[HALO - hw:HALO - hw + rows, cols]
        for o in range(-hw + 1, hw):
            ssum = ssum + p_ref[HALO + o:HALO + o + rows, cols]
        cnt = (jnp.minimum(t + hw, seq_len) - jnp.maximum(t - hw, 0)).astype(F32)
        dlt = ssum / cnt - p_ref[HALO:HALO + rows, cols]
        yp = jnp.dot(dlt.astype(BF16), wpool_ref[g], preferred_element_type=F32) * pscale_ref[:, cols]
        ypool_ref[:, cols] = (yp * _silu(gp_ref[:, cols].astype(F32))).astype(ypool_ref.dtype)


def _poolconv(proj, wpool, pscale, wdw, bdw, lng, lnb, wpw, *, seq_len, rows, col0, width, name):
    m_rows = proj.shape[0]
    tiles_per_seq = seq_len // rows
    hpr = rows // HALO
    n_halo = m_rows // HALO
    cb = col0 // width

    def cur(k):
        return pl.BlockSpec((rows, width), lambda i: (i, cb + k))

    def prev(k):
        return pl.BlockSpec((HALO, width), lambda i: (jnp.maximum(i * hpr - 1, 0), cb + k))

    def nxt(k):
        return pl.BlockSpec((HALO, width), lambda i: (jnp.minimum((i + 1) * hpr, n_halo - 1), cb + k))

    full = lambda a: pl.BlockSpec(a.shape, lambda i: (0,) * a.ndim)
    consts = [wpool, pscale, wdw, bdw, lng, lnb, wpw]
    kern = functools.partial(_poolconv_kernel, rows=rows, tiles_per_seq=tiles_per_seq, seq_len=seq_len,
                             row_blk=min(64, rows))
    return pl.pallas_call(
        kern,
        grid=(m_rows // rows,),
        in_specs=[cur(0), prev(0), nxt(0), cur(1), cur(2), prev(2), nxt(2), cur(3), prev(3), nxt(3), cur(4)]
                 + [full(a) for a in consts],
        out_specs=[pl.BlockSpec((rows, width), lambda i: (i, 0)),
                   pl.BlockSpec((rows, width), lambda i: (i, 0))],
        out_shape=[jax.ShapeDtypeStruct((m_rows, width), BF16),
                   jax.ShapeDtypeStruct((m_rows, width), BF16)],
        scratch_shapes=[pltpu.VMEM((rows + 2 * HALO, width), F32),
                        pltpu.VMEM((rows + 2 * HALO, width), F32),
                        pltpu.VMEM((rows, width), BF16)],
        compiler_params=_cparams(1),
        name=name,
    )(*([proj] * 11), *consts)


def _outproj_kernel(*refs, d_model, att_width, pool_width, final):
    if final:
        ya_ref, yp_ref, yc_ref, w_ref, x_ref, mod_ref, fg_ref, out_ref = refs
    else:
        ya_ref, yp_ref, yc_ref, w_ref, x_ref, mod_ref, out_ref = refs
    c1 = att_width
    c2 = att_width + pool_width
    y = jnp.dot(ya_ref[...], w_ref[0:c1, :], preferred_element_type=F32)
    y = y + jnp.dot(yp_ref[...], w_ref[c1:c2, :], preferred_element_type=F32)
    y = y + jnp.dot(yc_ref[...], w_ref[c2:, :], preferred_element_type=F32)
    gate = mod_ref[0][:, 2 * d_model:]
    xn = x_ref[...] + gate * y
    if final:
        ms = jnp.mean(xn * xn, axis=-1, keepdims=True)
        xn = xn * lax.rsqrt(ms + EPS) * fg_ref[...]
    out_ref[...] = xn


def _outproj(ya, yp, yc, w, x2d, mod3, final_g, *, tm, tiles_per_mod, mod_row0, name):
    m_rows, d = x2d.shape
    final = final_g is not None
    in_specs = [pl.BlockSpec((tm, ya.shape[1]), lambda i: (i, 0)),
                pl.BlockSpec((tm, yp.shape[1]), lambda i: (i, 0)),
                pl.BlockSpec((tm, yc.shape[1]), lambda i: (i, 0)),
                pl.BlockSpec(w.shape, lambda i: (0, 0)),
                pl.BlockSpec((tm, d), lambda i: (i, 0)),
                pl.BlockSpec((1, 1, mod3.shape[-1]), lambda i: (mod_row0 + i // tiles_per_mod, 0, 0))]
    args = [ya, yp, yc, w, x2d, mod3]
    if final:
        in_specs.append(pl.BlockSpec((1, d), lambda i: (0, 0)))
        args.append(final_g)
    kern = functools.partial(_outproj_kernel, d_model=d, att_width=ya.shape[1], pool_width=yp.shape[1],
                             final=final)
    return pl.pallas_call(
        kern,
        grid=(m_rows // tm,),
        in_specs=in_specs,
        out_specs=pl.BlockSpec((tm, d), lambda i: (i, 0)),
        out_shape=jax.ShapeDtypeStruct((m_rows, d), F32),
        compiler_params=_cparams(1),
        name=name,
    )(*args)


def _rope_tables(seq_len):
    n_freq = ATT_HD // 4
    inv_freq = ROPE_BASE ** (-jnp.arange(n_freq, dtype=F32) / n_freq)
    t = jnp.arange(seq_len, dtype=jnp.int32)
    lane = jnp.arange(LANES, dtype=jnp.int32)
    in_col_half = (lane % ATT_HD) >= ATT_HD // 2
    pos = jnp.where(in_col_half[None, :], (t % GRID_W)[:, None], (t // GRID_W)[:, None]).astype(F32)
    ang = pos * inv_freq[lane % n_freq][None, :]
    upper = ((lane % (2 * n_freq)) >= n_freq)[None, :]
    cos = jnp.cos(ang)
    sin = jnp.sin(ang)
    return cos, jnp.where(upper, 0.0, -sin), jnp.where(upper, sin, 0.0)


def kernel(x, c, ctx, c_ctx, w_mod, b_mod, norm_g, w_in, lambda_q1, lambda_k1, lambda_q2, lambda_k2,
           subln_g, w_pool, pool_scale, w_dw, b_dw, conv_ln_g, conv_ln_b, w_pw2, w_out, final_g):
    batch, seq, d = x.shape
    lc = ctx.shape[1]
    depth = w_mod.shape[0]
    att_w = d // 2
    pool_w = d // 4
    assert batch < 8 and seq % GRID_W == 0 and att_w == ATT_HEADS * ATT_VD and pool_w == 4 * POOL_GD
    assert seq % KV_CHUNK == 0 and lc % KV_CHUNK == 0

    cc = jnp.concatenate([c, c_ctx[None, :], jnp.zeros((8 - batch - 1, d), F32)], axis=0)
    lam_init = [0.8 - 0.6 * math.exp(-0.3 * l) for l in range(depth)]
    li = jnp.broadcast_to(jnp.asarray(lam_init, F32)[:, None, None], (depth, 1, LANES))
    mod, lam = _modulation(cc, w_mod, b_mod, lambda_q1, lambda_k1, lambda_q2, lambda_k2, li)
    tabs = _rope_tables(seq)

    tm_lat = min(1024, seq)
    tn = 512
    n_in = w_in.shape[-1]
    hcol = att_w // LANES
    pc_col0 = 4 * att_w
    x2d = x.reshape(batch * seq, d)
    c2d = ctx.reshape(batch * lc, d)
    row = lambda a: a.reshape(1, -1)

    for l in range(depth):
        last = l == depth - 1
        mod3 = mod[l].reshape(8, 1, 3 * d)
        w_in_l = w_in[l].astype(BF16)
        w_out_l = w_out[l].astype(BF16)
        lw = (w_pool[l].astype(BF16), row(pool_scale[l]), w_dw[l], row(b_dw[l]), row(conv_ln_g[l]),
              row(conv_ln_b[l]), w_pw2[l].astype(BF16))
        g_l = row(norm_g[l])
        out_scale = 1.0 - lam_init[l]

        proj = _inproj(x2d, mod3, g_l, w_in_l, tabs, tm=tm_lat, tn=tn, tiles_per_mod=seq // tm_lat,
                       mod_row0=0, n0=0, n_tiles=n_in // tn, att_width=att_w, name=f"inproj_lat{l}")
        if last:
            cproj = _inproj(c2d, mod3, g_l, w_in_l, None, tm=batch * lc, tn=tn, tiles_per_mod=1,
                            mod_row0=batch, n0=att_w // tn, n_tiles=2 * att_w // tn, att_width=att_w,
                            name=f"inproj_ctx{l}")
            kc_col, vc_col = 0, hcol
        else:
            cproj = _inproj(c2d, mod3, g_l, w_in_l, None, tm=batch * lc, tn=tn, tiles_per_mod=1,
                            mod_row0=batch, n0=0, n_tiles=n_in // tn, att_width=att_w,
                            name=f"inproj_ctx{l}")
            kc_col, vc_col = hcol, 2 * hcol

        y_att = _attention(proj, cproj, proj, lam[l:l + 1], row(subln_g[l]), batch=batch, lq=seq,
                           tq=min(512, seq), q_col=0, g_col=3 * hcol, lc=lc, kc_col=kc_col, vc_col=vc_col,
                           ll=seq, kl_col=hcol, vl_col=2 * hcol, out_scale=out_scale, name=f"attn_lat{l}")
        y_pool, y_conv = _poolconv(proj, *lw, seq_len=seq, rows=min(256, seq), col0=pc_col0, width=pool_w,
                                   name=f"poolconv_lat{l}")
        if not last:
            yc_att = _attention(cproj, cproj, None, lam[l:l + 1], row(subln_g[l]), batch=batch, lq=lc,
                                tq=lc, q_col=0, g_col=3 * hcol, lc=lc, kc_col=kc_col, vc_col=vc_col,
                                ll=0, kl_col=0, vl_col=0, out_scale=out_scale, name=f"attn_ctx{l}")
            yc_pool, yc_conv = _poolconv(cproj, *lw, seq_len=lc, rows=min(256, lc), col0=pc_col0,
                                         width=pool_w, name=f"poolconv_ctx{l}")
            c2d = _outproj(yc_att, yc_pool, yc_conv, w_out_l, c2d, mod3, None, tm=min(512, batch * lc),
                           tiles_per_mod=batch * lc, mod_row0=batch, name=f"outproj_ctx{l}")
        tm_o = min(512, seq)
        x2d = _outproj(y_att, y_pool, y_conv, w_out_l, x2d, mod3, row(final_g) if last else None,
                       tm=tm_o, tiles_per_mod=seq // tm_o, mod_row0=0, name=f"outproj_lat{l}")
    return x2d.reshape(batch, seq, d)
```

```python
import functools
import math

import jax
import jax.numpy as jnp
from jax import lax
from jax.experimental import pallas as pl
from jax.experimental.pallas import tpu as pltpu

ATT_HEADS = 8
ATT_HD = 64
ATT_VD = 2 * ATT_HD
POOL_WINDOWS = (2, 4, 8, 16)
POOL_GD = 128
CONV_K = 31
GRID_W = 64
ROPE_BASE = 10000.0
EPS = 1e-6

LANES = 128
KV_CHUNK = 256
Q_SUB = 128
TICKS_PER_TRIP = 14
HALO = 16
VMEM_LIMIT = 56 * 1024 * 1024

F32 = jnp.float32
BF16 = jnp.bfloat16


def _cparams(n_axes, flags=None):
    return pltpu.CompilerParams(dimension_semantics=("arbitrary",) * n_axes,
                                vmem_limit_bytes=VMEM_LIMIT, flags=flags)


def _silu(v):
    return v * jax.nn.sigmoid(v)


def _mod_kernel(cc_ref, w_ref, b_ref, lq1_ref, lk1_ref, lq2_ref, lk2_ref, li_ref, out_ref, lam_ref):
    s = _silu(cc_ref[...])
    out_ref[0] = jnp.dot(s, w_ref[0], precision=lax.Precision.HIGHEST,
                         preferred_element_type=F32) + b_ref[0]
    a1 = jnp.sum(lq1_ref[0] * lk1_ref[0], axis=-1, keepdims=True)
    a2 = jnp.sum(lq2_ref[0] * lk2_ref[0], axis=-1, keepdims=True)
    lam = jnp.exp(a1) - jnp.exp(a2) + li_ref[0]
    lam_ref[0] = jnp.broadcast_to(lam, lam_ref.shape[1:])


def _modulation(cc, w_mod, b_mod, lq1, lk1, lq2, lk2, lam_init):
    depth, d, n = w_mod.shape
    tn = n // 4 if n % (4 * LANES) == 0 else n
    vec = lambda a: a.reshape(depth, 1, a.shape[-1])
    lspec = pl.BlockSpec((1, 1, ATT_HD), lambda l, j: (l, 0, 0))
    return pl.pallas_call(
        _mod_kernel,
        grid=(depth, n // tn),
        in_specs=[pl.BlockSpec((8, d), lambda l, j: (0, 0)),
                  pl.BlockSpec((1, d, tn), lambda l, j: (l, 0, j)),
                  pl.BlockSpec((1, 1, tn), lambda l, j: (l, 0, j)),
                  lspec, lspec, lspec, lspec,
                  pl.BlockSpec((1, 1, LANES), lambda l, j: (l, 0, 0))],
        out_specs=[pl.BlockSpec((1, 8, tn), lambda l, j: (l, 0, j)),
                   pl.BlockSpec((1, 8, LANES), lambda l, j: (l, 0, 0))],
        out_shape=[jax.ShapeDtypeStruct((depth, 8, n), F32),
                   jax.ShapeDtypeStruct((depth, 8, LANES), F32)],
        compiler_params=_cparams(2),
        name="modulation",
    )(cc, w_mod, vec(b_mod), vec(lq1), vec(lk1), vec(lq2), vec(lk2), lam_init)


def _inproj_kernel(*refs, d_model, rope, n_rope_tiles, n_q_tiles):
    if rope:
        x_ref, mod_ref, g_ref, w_ref, cos_ref, sa_ref, sb_ref, out_ref, h_ref = refs
    else:
        x_ref, mod_ref, g_ref, w_ref, out_ref, h_ref = refs
    j = pl.program_id(1)

    @pl.when(j == 0)
    def _():
        xf = x_ref[...]
        ms = jnp.mean(xf * xf, axis=-1, keepdims=True)
        m = mod_ref[0]
        shift = m[:, :d_model]
        scale = m[:, d_model:2 * d_model]
        h = xf * lax.rsqrt(ms + EPS) * g_ref[...] * (1.0 + scale) + shift
        h_ref[...] = h.astype(BF16)

    acc = jnp.dot(h_ref[...], w_ref[...], preferred_element_type=F32)
    qs = jnp.where(j < n_q_tiles, ATT_HD ** -0.5 * math.log2(math.e), 1.0).astype(F32)
    if not rope:
        out_ref[...] = (acc * qs).astype(out_ref.dtype)
        return

    @pl.when(j < n_rope_tiles)
    def _():
        cos = cos_ref[...] * qs
        sa = sa_ref[...] * qs
        sb = sb_ref[...] * qs
        for hh in range(acc.shape[1] // LANES):
            t = acc[:, hh * LANES:(hh + 1) * LANES]
            r = (t * cos + pltpu.roll(t, LANES - ATT_HD // 4, 1) * sa
                 + pltpu.roll(t, ATT_HD // 4, 1) * sb)
            out_ref[:, hh * LANES:(hh + 1) * LANES] = r.astype(out_ref.dtype)

    @pl.when(j >= n_rope_tiles)
    def _():
        out_ref[...] = acc.astype(out_ref.dtype)


def _inproj(x2d, mod3, g, w, tabs, *, tm, tn, tiles_per_mod, mod_row0, n0, n_tiles, att_width, name):
    m_rows, d = x2d.shape
    rope = tabs is not None
    tiles_per_seq = None
    in_specs = [pl.BlockSpec((tm, d), lambda i, j: (i, 0)),
                pl.BlockSpec((1, 1, mod3.shape[-1]), lambda i, j: (mod_row0 + i // tiles_per_mod, 0, 0)),
                pl.BlockSpec((1, d), lambda i, j: (0, 0)),
                pl.BlockSpec((d, tn), lambda i, j: (0, n0 + j))]
    args = [x2d, mod3, g, w]
    if rope:
        tiles_per_seq = tabs[0].shape[0] // tm
        tspec = pl.BlockSpec((tm, LANES), lambda i, j: (i % tiles_per_seq, 0))
        in_specs += [tspec, tspec, tspec]
        args += list(tabs)
    kern = functools.partial(_inproj_kernel, d_model=d, rope=rope,
                             n_rope_tiles=max(2 * att_width // tn - n0, 0),
                             n_q_tiles=max(att_width // tn - n0, 0))
    return pl.pallas_call(
        kern,
        grid=(m_rows // tm, n_tiles),
        in_specs=in_specs,
        out_specs=pl.BlockSpec((tm, tn), lambda i, j: (i, j)),
        out_shape=jax.ShapeDtypeStruct((m_rows, n_tiles * tn), BF16),
        scratch_shapes=[pltpu.VMEM((tm, d), BF16)],
        compiler_params=_cparams(2),
        name=name,
    )(*args)


def _attn_kernel(*refs, lc, ll, n_sub, out_scale):
    if ll:
        q_ref, kc_ref, vc_ref, kl_ref, vl_ref, g_ref, lam_ref, sg_ref, out_ref, vt_ref, *slots = refs
        k_parts = [(kc_ref, 0, lc), (kl_ref, lc, ll)]
        v_parts = [(vc_ref, 0, lc), (vl_ref, lc, ll)]
    else:
        q_ref, kc_ref, vc_ref, g_ref, lam_ref, sg_ref, out_ref, vt_ref, *slots = refs
        k_parts = [(kc_ref, 0, lc)]
        v_parts = [(vc_ref, 0, lc)]
    lk = lc + ll
    st_ref = slots[0:2]
    pt_ref = slots[2:4]

    @pl.when(pl.program_id(2) == 0)
    def _():
        for ref, off, n in v_parts:
            for c in range(n // KV_CHUNK):
                blk = ref[c * KV_CHUNK:(c + 1) * KV_CHUNK, :].astype(F32)
                vt_ref[:, off + c * KV_CHUNK:off + (c + 1) * KV_CHUNK] = blk.T.astype(BF16)

    row = lax.broadcasted_iota(jnp.int32, (ATT_VD, Q_SUB), 0)
    lam = lam_ref[0][0:1, :]
    sg = sg_ref[...]

    def rows_of(t):
        if isinstance(t, int):
            return pl.ds(t * Q_SUB, Q_SUB)
        return pl.ds(pl.multiple_of(t * Q_SUB, Q_SUB), Q_SUB)

    def scores(t, slot):
        qt = q_ref[rows_of(t), :].astype(F32).T
        qbd = jnp.concatenate([jnp.where(row < ATT_HD, qt, 0.0),
                               jnp.where(row >= ATT_HD, qt, 0.0)], axis=1).astype(BF16)
        m8 = None
        for k_ref, off, n in k_parts:
            for c in range(n // KV_CHUNK):
                kch = k_ref[c * KV_CHUNK:(c + 1) * KV_CHUNK, :]
                st = jnp.dot(kch, qbd, preferred_element_type=F32)
                st_ref[slot][off + c * KV_CHUNK:off + (c + 1) * KV_CHUNK, :] = st
                for r in range(KV_CHUNK // 8):
                    piece = st[r * 8:(r + 1) * 8, :]
                    m8 = piece if m8 is None else jnp.maximum(m8, piece)
        return jnp.max(m8, axis=0, keepdims=True)

    def softmax(slot, m):
        l8 = None
        for c in range(lk // KV_CHUNK):
            rows = slice(c * KV_CHUNK, (c + 1) * KV_CHUNK)
            p = jnp.exp2(st_ref[slot][rows, :] - m)
            for r in range(KV_CHUNK // 8):
                piece = p[r * 8:(r + 1) * 8, :]
                l8 = piece if l8 is None else l8 + piece
            pt_ref[slot][rows, :] = p.astype(BF16)
        return l8

    def pv(t, slot, l8):
        acc = jnp.dot(vt_ref[...], pt_ref[slot][...], preferred_element_type=F32)
        o = acc * (1.0 / jnp.sum(l8, axis=0, keepdims=True))
        ot = o[:, :Q_SUB] - lam * o[:, Q_SUB:]
        oq = ot.T
        ms = jnp.mean(oq * oq, axis=-1, keepdims=True)
        y = oq * lax.rsqrt(ms + EPS) * sg * out_scale
        gate = g_ref[rows_of(t), :].astype(F32)
        out_ref[rows_of(t), :] = (y * _silu(gate)).astype(out_ref.dtype)

    def tick(t, slot, carry, do_scores=True, do_softmax=True, do_pv=True):
        m_prev, l8_prev = carry
        m = scores(t, slot) if do_scores else None
        l8 = softmax(1 - slot, m_prev) if do_softmax else None
        if do_pv:
            pv(t - 2, slot, l8_prev)
        return m, l8

    n_trips = (n_sub - 2) // TICKS_PER_TRIP if n_sub - 2 >= 2 * TICKS_PER_TRIP else 0
    n_loop = n_trips * TICKS_PER_TRIP
    carry = (None, None)
    t = 0
    while t < 2:
        carry = tick(t, t % 2, carry, do_scores=t < n_sub, do_softmax=1 <= t <= n_sub, do_pv=False)
        t += 1
    if n_loop:
        def trip(j, c):
            t0 = 2 + TICKS_PER_TRIP * j
            for u in range(TICKS_PER_TRIP):
                c = tick(t0 + u, u % 2, c)
            return c
        carry = lax.fori_loop(0, n_trips, trip, carry)
        t += n_loop
    while t < n_sub + 2:
        carry = tick(t, t % 2, carry, do_scores=t < n_sub, do_softmax=1 <= t <= n_sub, do_pv=True)
        t += 1


def _attention(q_arr, kv_ctx, kv_lat, lam, subln_g, *, batch, lq, tq, q_col, g_col, lc, kc_col, vc_col,
               ll, kl_col, vl_col, out_scale, name):
    nq = lq // tq
    hb = ATT_HEADS
    in_specs = [pl.BlockSpec((tq, LANES), lambda b, h, i: (b * nq + i, q_col + h)),
                pl.BlockSpec((lc, LANES), lambda b, h, i: (b, kc_col + h)),
                pl.BlockSpec((lc, LANES), lambda b, h, i: (b, vc_col + h))]
    args = [q_arr, kv_ctx, kv_ctx]
    if kv_lat is not None:
        in_specs += [pl.BlockSpec((ll, LANES), lambda b, h, i: (b, kl_col + h)),
                     pl.BlockSpec((ll, LANES), lambda b, h, i: (b, vl_col + h))]
        args += [kv_lat, kv_lat]
    else:
        ll = 0
    in_specs += [pl.BlockSpec((tq, LANES), lambda b, h, i: (b * nq + i, g_col + h)),
                 pl.BlockSpec((1, 8, LANES), lambda b, h, i: (0, 0, 0)),
                 pl.BlockSpec((1, ATT_VD), lambda b, h, i: (0, 0))]
    args += [q_arr, lam, subln_g]
    kern = functools.partial(_attn_kernel, lc=lc, ll=ll, n_sub=tq // Q_SUB, out_scale=out_scale)
    return pl.pallas_call(
        kern,
        grid=(batch, hb, nq),
        in_specs=in_specs,
        out_specs=pl.BlockSpec((tq, LANES), lambda b, h, i: (b * nq + i, h)),
        out_shape=jax.ShapeDtypeStruct((batch * lq, hb * ATT_VD), BF16),
        scratch_shapes=[pltpu.VMEM((ATT_VD, lc + ll), BF16),
                        pltpu.VMEM((lc + ll, 2 * Q_SUB), F32), pltpu.VMEM((lc + ll, 2 * Q_SUB), F32),
                        pltpu.VMEM((lc + ll, 2 * Q_SUB), BF16), pltpu.VMEM((lc + ll, 2 * Q_SUB), BF16)],
        compiler_params=_cparams(3),
        name=name,
    )(*args)


def _poolconv_kernel(up_c, up_p, up_n, gp_ref, a_c, a_p, a_n, b_c, b_p, b_n, gc_ref,
                     wpool_ref, pscale_ref, wdw_ref, bdw_ref, lng_ref, lnb_ref, wpw_ref,
                     ypool_ref, yconv_ref, u_ref, p_ref, z_ref, *, rows, tiles_per_seq, seq_len, row_blk):
    iseq = pl.program_id(0) % tiles_per_seq
    has_prev = iseq > 0
    has_next = iseq < tiles_per_seq - 1

    def glu(a_ref, b_ref):
        return a_ref[...].astype(F32) * jax.nn.sigmoid(b_ref[...].astype(F32))

    u_ref[0:HALO, :] = jnp.where(has_prev, glu(a_p, b_p), 0.0)
    u_ref[HALO:HALO + rows, :] = glu(a_c, b_c)
    u_ref[HALO + rows:, :] = jnp.where(has_next, glu(a_n, b_n), 0.0)
    p_ref[0:HALO, :] = jnp.where(has_prev, up_p[...].astype(F32), 0.0)
    p_ref[HALO:HALO + rows, :] = up_c[...].astype(F32)
    p_ref[HALO + rows:, :] = jnp.where(has_next, up_n[...].astype(F32), 0.0)

    half = CONV_K // 2
    for rb in range(rows // row_blk):
        base = HALO + rb * row_blk - half
        acc = u_ref[base:base + row_blk, :] * wdw_ref[0:1, :]
        for j in range(1, CONV_K):
            acc = acc + u_ref[base + j:base + j + row_blk, :] * wdw_ref[j:j + 1, :]
        y = acc + bdw_ref[...]
        mu = jnp.mean(y, axis=-1, keepdims=True)
        yc = y - mu
        var = jnp.mean(yc * yc, axis=-1, keepdims=True)
        z = yc * lax.rsqrt(var + EPS) * lng_ref[...] + lnb_ref[...]
        z_ref[rb * row_blk:(rb + 1) * row_blk, :] = _silu(z).astype(BF16)
    yconv = jnp.dot(z_ref[...], wpw_ref[...], preferred_element_type=F32)
    yconv_ref[...] = (yconv * _silu(gc_ref[...].astype(F32))).astype(yconv_ref.dtype)

    t = iseq * rows + lax.broadcasted_iota(jnp.int32, (rows, POOL_GD), 0)
    for g, w in enumerate(POOL_WINDOWS):
        cols = slice(g * POOL_GD, (g + 1) * POOL_GD)
        hw = w // 2
        ssum = p_ref[HALO - hw:HALO - hw + rows, cols]
        for o in range(-hw + 1, hw):
            ssum = ssum + p_ref[HALO + o:HALO + o + rows, cols]
        cnt = (jnp.minimum(t + hw, seq_len) - jnp.maximum(t - hw, 0)).astype(F32)
        dlt = ssum / cnt - p_ref[HALO:HALO + rows, cols]
        yp = jnp.dot(dlt.astype(BF16), wpool_ref[g], preferred_element_type=F32) * pscale_ref[:, cols]
        ypool_ref[:, cols] = (yp * _silu(gp_ref[:, cols].astype(F32))).astype(ypool_ref.dtype)


def _poolconv(proj, wpool, pscale, wdw, bdw, lng, lnb, wpw, *, seq_len, rows, col0, width, name):
    m_rows = proj.shape[0]
    tiles_per_seq = seq_len // rows
    hpr = rows // HALO
    n_halo = m_rows // HALO
    cb = col0 // width

    def cur(k):
        return pl.BlockSpec((rows, width), lambda i: (i, cb + k))

    def prev(k):
        return pl.BlockSpec((HALO, width), lambda i: (jnp.maximum(i * hpr - 1, 0), cb + k))

    def nxt(k):
        return pl.BlockSpec((HALO, width), lambda i: (jnp.minimum((i + 1) * hpr, n_halo - 1), cb + k))

    full = lambda a: pl.BlockSpec(a.shape, lambda i: (0,) * a.ndim)
    consts = [wpool, pscale, wdw, bdw, lng, lnb, wpw]
    kern = functools.partial(_poolconv_kernel, rows=rows, tiles_per_seq=tiles_per_seq, seq_len=seq_len,
                             row_blk=min(64, rows))
    return pl.pallas_call(
        kern,
        grid=(m_rows // rows,),
        in_specs=[cur(0), prev(0), nxt(0), cur(1), cur(2), prev(2), nxt(2), cur(3), prev(3), nxt(3), cur(4)]
                 + [full(a) for a in consts],
        out_specs=[pl.BlockSpec((rows, width), lambda i: (i, 0)),
                   pl.BlockSpec((rows, width), lambda i: (i, 0))],
        out_shape=[jax.ShapeDtypeStruct((m_rows, width), BF16),
                   jax.ShapeDtypeStruct((m_rows, width), BF16)],
        scratch_shapes=[pltpu.VMEM((rows + 2 * HALO, width), F32),
                        pltpu.VMEM((rows + 2 * HALO, width), F32),
                        pltpu.VMEM((rows, width), BF16)],
        compiler_params=_cparams(1),
        name=name,
    )(*([proj] * 11), *consts)


def _outproj_kernel(*refs, d_model, att_width, pool_width, final):
    if final:
        ya_ref, yp_ref, yc_ref, w_ref, x_ref, mod_ref, fg_ref, out_ref = refs
    else:
        ya_ref, yp_ref, yc_ref, w_ref, x_ref, mod_ref, out_ref = refs
    c1 = att_width
    c2 = att_width + pool_width
    y = jnp.dot(ya_ref[...], w_ref[0:c1, :], preferred_element_type=F32)
    y = y + jnp.dot(yp_ref[...], w_ref[c1:c2, :], preferred_element_type=F32)
    y = y + jnp.dot(yc_ref[...], w_ref[c2:, :], preferred_element_type=F32)
    gate = mod_ref[0][:, 2 * d_model:]
    xn = x_ref[...] + gate * y
    if final:
        ms = jnp.mean(xn * xn, axis=-1, keepdims=True)
        xn = xn * lax.rsqrt(ms + EPS) * fg_ref[...]
    out_ref[...] = xn


def _outproj(ya, yp, yc, w, x2d, mod3, final_g, *, tm, tiles_per_mod, mod_row0, name):
    m_rows, d = x2d.shape
    final = final_g is not None
    in_specs = [pl.BlockSpec((tm, ya.shape[1]), lambda i: (i, 0)),
                pl.BlockSpec((tm, yp.shape[1]), lambda i: (i, 0)),
                pl.BlockSpec((tm, yc.shape[1]), lambda i: (i, 0)),
                pl.BlockSpec(w.shape, lambda i: (0, 0)),
                pl.BlockSpec((tm, d), lambda i: (i, 0)),
                pl.BlockSpec((1, 1, mod3.shape[-1]), lambda i: (mod_row0 + i // tiles_per_mod, 0, 0))]
    args = [ya, yp, yc, w, x2d, mod3]
    if final:
        in_specs.append(pl.BlockSpec((1, d), lambda i: (0, 0)))
        args.append(final_g)
    kern = functools.partial(_outproj_kernel, d_model=d, att_width=ya.shape[1], pool_width=yp.shape[1],
                             final=final)
    return pl.pallas_call(
        kern,
        grid=(m_rows // tm,),
        in_specs=in_specs,
        out_specs=pl.BlockSpec((tm, d), lambda i: (i, 0)),
        out_shape=jax.ShapeDtypeStruct((m_rows, d), F32),
        compiler_params=_cparams(1),
        name=name,
    )(*args)


def _rope_tables(seq_len):
    n_freq = ATT_HD // 4
    inv_freq = ROPE_BASE ** (-jnp.arange(n_freq, dtype=F32) / n_freq)
    t = jnp.arange(seq_len, dtype=jnp.int32)
    lane = jnp.arange(LANES, dtype=jnp.int32)
    in_col_half = (lane % ATT_HD) >= ATT_HD // 2
    pos = jnp.where(in_col_half[None, :], (t % GRID_W)[:, None], (t // GRID_W)[:, None]).astype(F32)
    ang = pos * inv_freq[lane % n_freq][None, :]
    upper = ((lane % (2 * n_freq)) >= n_freq)[None, :]
    cos = jnp.cos(ang)
    sin = jnp.sin(ang)
    return cos, jnp.where(upper, 0.0, -sin), jnp.where(upper, sin, 0.0)


def kernel(x, c, ctx, c_ctx, w_mod, b_mod, norm_g, w_in, lambda_q1, lambda_k1, lambda_q2, lambda_k2,
           subln_g, w_pool, pool_scale, w_dw, b_dw, conv_ln_g, conv_ln_b, w_pw2, w_out, final_g):
    batch, seq, d = x.shape
    lc = ctx.shape[1]
    depth = w_mod.shape[0]
    att_w = d // 2
    pool_w = d // 4
    assert batch < 8 and seq % GRID_W == 0 and att_w == ATT_HEADS * ATT_VD and pool_w == 4 * POOL_GD
    assert seq % KV_CHUNK == 0 and lc % KV_CHUNK == 0

    cc = jnp.concatenate([c, c_ctx[None, :], jnp.zeros((8 - batch - 1, d), F32)], axis=0)
    lam_init = [0.8 - 0.6 * math.exp(-0.3 * l) for l in range(depth)]
    li = jnp.broadcast_to(jnp.asarray(lam_init, F32)[:, None, None], (depth, 1, LANES))
    mod, lam = _modulation(cc, w_mod, b_mod, lambda_q1, lambda_k1, lambda_q2, lambda_k2, li)
    tabs = _rope_tables(seq)

    tm_lat = min(1024, seq)
    tn = 512
    n_in = w_in.shape[-1]
    hcol = att_w // LANES
    pc_col0 = 4 * att_w
    x2d = x.reshape(batch * seq, d)
    c2d = ctx.reshape(batch * lc, d)
    row = lambda a: a.reshape(1, -1)

    for l in range(depth):
        last = l == depth - 1
        mod3 = mod[l].reshape(8, 1, 3 * d)
        w_in_l = w_in[l].astype(BF16)
        w_out_l = w_out[l].astype(BF16)
        lw = (w_pool[l].astype(BF16), row(pool_scale[l]), w_dw[l], row(b_dw[l]), row(conv_ln_g[l]),
              row(conv_ln_b[l]), w_pw2[l].astype(BF16))
        g_l = row(norm_g[l])
        out_scale = 1.0 - lam_init[l]

        proj = _inproj(x2d, mod3, g_l, w_in_l, tabs, tm=tm_lat, tn=tn, tiles_per_mod=seq // tm_lat,
                       mod_row0=0, n0=0, n_tiles=n_in // tn, att_width=att_w, name=f"inproj_lat{l}")
        if last:
            cproj = _inproj(c2d, mod3, g_l, w_in_l, None, tm=batch * lc, tn=tn, tiles_per_mod=1,
                            mod_row0=batch, n0=att_w // tn, n_tiles=2 * att_w // tn, att_width=att_w,
                            name=f"inproj_ctx{l}")
            kc_col, vc_col = 0, hcol
        else:
            cproj = _inproj(c2d, mod3, g_l, w_in_l, None, tm=batch * lc, tn=tn, tiles_per_mod=1,
                            mod_row0=batch, n0=0, n_tiles=n_in // tn, att_width=att_w,
                            name=f"inproj_ctx{l}")
            kc_col, vc_col = hcol, 2 * hcol

        y_att = _attention(proj, cproj, proj, lam[l:l + 1], row(subln_g[l]), batch=batch, lq=seq,
                           tq=seq, q_col=0, g_col=3 * hcol, lc=lc, kc_col=kc_col, vc_col=vc_col,
                           ll=seq, kl_col=hcol, vl_col=2 * hcol, out_scale=out_scale, name=f"attn_lat{l}")
        y_pool, y_conv = _poolconv(proj, *lw, seq_len=seq, rows=min(256, seq), col0=pc_col0, width=pool_w,
                                   name=f"poolconv_lat{l}")
        if not last:
            yc_att = _attention(cproj, cproj, None, lam[l:l + 1], row(subln_g[l]), batch=batch, lq=lc,
                                tq=lc, q_col=0, g_col=3 * hcol, lc=lc, kc_col=kc_col, vc_col=vc_col,
                                ll=0, kl_col=0, vl_col=0, out_scale=out_scale, name=f"attn_ctx{l}")
            yc_pool, yc_conv = _poolconv(cproj, *lw, seq_len=lc, rows=min(256, lc), col0=pc_col0,
                                         width=pool_w, name=f"poolconv_ctx{l}")
            c2d = _outproj(yc_att, yc_pool, yc_conv, w_out_l, c2d, mod3, None, tm=min(512, batch * lc),
                           tiles_per_mod=batch * lc, mod_row0=batch, name=f"outproj_ctx{l}")
        tm_o = min(512, seq)
        x2d = _outproj(y_att, y_pool, y_conv, w_out_l, x2d, mod3, row(final_g) if last else None,
                       tm=tm_o, tiles_per_mod=seq // tm_o, mod_row0=0, name=f"outproj_lat{l}")
    return x2d.reshape(batch, seq, d)
```

```python
import functools
import math

import jax
import jax.numpy as jnp
import numpy as np
from jax import lax
from jax.experimental import pallas as pl
from jax.experimental.pallas import tpu as pltpu

ATT_HEADS = 8
ATT_HD = 64
ATT_VD = 2 * ATT_HD
POOL_WINDOWS = (2, 4, 8, 16)
POOL_GD = 128
CONV_K = 31
GRID_W = 64
ROPE_BASE = 10000.0
EPS = 1e-6

LANES = 128
KV_CHUNK = 256
Q_SUB = 128
TICKS_PER_TRIP = 14
HALO = 16
VMEM_LIMIT = 56 * 1024 * 1024

F32 = jnp.float32
BF16 = jnp.bfloat16


def _cparams(n_axes, flags=None):
    return pltpu.CompilerParams(dimension_semantics=("arbitrary",) * n_axes,
                                vmem_limit_bytes=VMEM_LIMIT, flags=flags)


def _silu(v):
    return v * jax.nn.sigmoid(v)


def _mod_kernel(cc_ref, w_ref, b_ref, lq1_ref, lk1_ref, lq2_ref, lk2_ref, li_ref, out_ref, lam_ref):
    s = _silu(cc_ref[...])
    out_ref[0] = jnp.dot(s, w_ref[0], precision=lax.Precision.HIGHEST,
                         preferred_element_type=F32) + b_ref[0]
    a1 = jnp.sum(lq1_ref[0] * lk1_ref[0], axis=-1, keepdims=True)
    a2 = jnp.sum(lq2_ref[0] * lk2_ref[0], axis=-1, keepdims=True)
    lam = jnp.exp(a1) - jnp.exp(a2) + li_ref[0]
    lam_ref[0] = jnp.broadcast_to(lam, lam_ref.shape[1:])


def _modulation(cc, w_mod, b_mod, lq1, lk1, lq2, lk2, lam_init):
    depth, d, n = w_mod.shape
    tn = n // 4 if n % (4 * LANES) == 0 else n
    vec = lambda a: a.reshape(depth, 1, a.shape[-1])
    lspec = pl.BlockSpec((1, 1, ATT_HD), lambda l, j: (l, 0, 0))
    return pl.pallas_call(
        _mod_kernel,
        grid=(depth, n // tn),
        in_specs=[pl.BlockSpec((8, d), lambda l, j: (0, 0)),
                  pl.BlockSpec((1, d, tn), lambda l, j: (l, 0, j)),
                  pl.BlockSpec((1, 1, tn), lambda l, j: (l, 0, j)),
                  lspec, lspec, lspec, lspec,
                  pl.BlockSpec((1, 1, LANES), lambda l, j: (l, 0, 0))],
        out_specs=[pl.BlockSpec((1, 8, tn), lambda l, j: (l, 0, j)),
                   pl.BlockSpec((1, 8, LANES), lambda l, j: (l, 0, 0))],
        out_shape=[jax.ShapeDtypeStruct((depth, 8, n), F32),
                   jax.ShapeDtypeStruct((depth, 8, LANES), F32)],
        compiler_params=_cparams(2),
        name="modulation",
    )(cc, w_mod, vec(b_mod), vec(lq1), vec(lk1), vec(lq2), vec(lk2), lam_init)


def _inproj_kernel(*refs, d_model, rope, n_rope_tiles, n_q_tiles, row_chunk):
    if rope:
        x_ref, mod_ref, g_ref, w_ref, cos_ref, sin_ref, perm_ref, out_ref, h_ref = refs
    else:
        x_ref, mod_ref, g_ref, w_ref, out_ref, h_ref = refs
    j = pl.program_id(1)
    tm, tn = out_ref.shape

    def rope_store(rows, acc):
        cos = cos_ref[rows, :]
        sin = sin_ref[rows, :]
        for hh in range(tn // LANES):
            t = acc[:, hh * LANES:(hh + 1) * LANES]
            rot = jnp.dot(t.astype(BF16), perm_ref[...], preferred_element_type=F32)
            out_ref[rows, hh * LANES:(hh + 1) * LANES] = (t * cos + rot * sin).astype(out_ref.dtype)

    def plain_store(rows, acc):
        qs = jnp.where(j < n_q_tiles, ATT_HD ** -0.5 * math.log2(math.e), 1.0).astype(F32)
        out_ref[rows, :] = (acc * qs).astype(out_ref.dtype)

    @pl.when(j == 0)
    def _():
        m = mod_ref[0]
        shift = m[:, :d_model]
        gain = g_ref[...] * (1.0 + m[:, d_model:2 * d_model])
        for c in range(tm // row_chunk):
            rows = slice(c * row_chunk, (c + 1) * row_chunk)
            xf = x_ref[rows, :]
            ms = jnp.mean(xf * xf, axis=-1, keepdims=True)
            h = (xf * lax.rsqrt(ms + EPS) * gain + shift).astype(BF16)
            h_ref[rows, :] = h
            acc = jnp.dot(h, w_ref[...], preferred_element_type=F32)
            if rope:
                rope_store(rows, acc)
            else:
                plain_store(rows, acc)

    if rope and n_rope_tiles > 1:
        @pl.when(jnp.logical_and(j > 0, j < n_rope_tiles))
        def _():
            rope_store(slice(None), jnp.dot(h_ref[...], w_ref[...], preferred_element_type=F32))

    @pl.when(j >= (n_rope_tiles if rope else 1))
    def _():
        plain_store(slice(None), jnp.dot(h_ref[...], w_ref[...], preferred_element_type=F32))


def _inproj(x2d, mod3, g, w, tabs, *, layer, tm, tn, tiles_per_mod, mod_row0, n0, n_tiles, att_width, name):
    m_rows, d = x2d.shape
    rope = tabs is not None
    n_q_tiles = max(att_width // tn - n0, 0)
    in_specs = [pl.BlockSpec((tm, d), lambda i, j: (i, 0)),
                pl.BlockSpec((1, 1, mod3.shape[-1]), lambda i, j: (mod_row0 + i // tiles_per_mod, 0, 0)),
                pl.BlockSpec((1, d), lambda i, j: (0, 0)),
                pl.BlockSpec((None, d, tn), lambda i, j: (layer, 0, n0 + j))]
    args = [x2d, mod3, g, w]
    if rope:
        tiles_per_seq = tabs[0].shape[1] // tm
        tspec = pl.BlockSpec((None, tm, LANES),
                             lambda i, j: (jnp.where(j < n_q_tiles, 0, 1), i % tiles_per_seq, 0))
        in_specs += [tspec, tspec, pl.BlockSpec((LANES, LANES), lambda i, j: (0, 0))]
        args += list(tabs)
    kern = functools.partial(_inproj_kernel, d_model=d, rope=rope,
                             n_rope_tiles=max(2 * att_width // tn - n0, 0), n_q_tiles=n_q_tiles,
                             row_chunk=min(256, tm))
    return pl.pallas_call(
        kern,
        grid=(m_rows // tm, n_tiles),
        in_specs=in_specs,
        out_specs=pl.BlockSpec((tm, tn), lambda i, j: (i, j)),
        out_shape=jax.ShapeDtypeStruct((m_rows, n_tiles * tn), BF16),
        scratch_shapes=[pltpu.VMEM((tm, d), BF16)],
        compiler_params=_cparams(2),
        name=name,
    )(*args)


def _attn_kernel(*refs, lc, ll, n_sub, out_scale):
    if ll:
        q_ref, kc_ref, vc_ref, kl_ref, vl_ref, g_ref, lam_ref, sg_ref, out_ref, vt_ref, *slots = refs
        k_parts = [(kc_ref, 0, lc), (kl_ref, lc, ll)]
        v_parts = [(vc_ref, 0, lc), (vl_ref, lc, ll)]
    else:
        q_ref, kc_ref, vc_ref, g_ref, lam_ref, sg_ref, out_ref, vt_ref, *slots = refs
        k_parts = [(kc_ref, 0, lc)]
        v_parts = [(vc_ref, 0, lc)]
    lk = lc + ll
    st_ref = slots[0:2]
    pt_ref = slots[2:4]

    @pl.when(pl.program_id(2) == 0)
    def _():
        for ref, off, n in v_parts:
            for c in range(n // KV_CHUNK):
                blk = ref[c * KV_CHUNK:(c + 1) * KV_CHUNK, :].astype(F32)
                vt_ref[:, off + c * KV_CHUNK:off + (c + 1) * KV_CHUNK] = blk.T.astype(BF16)

    row = lax.broadcasted_iota(jnp.int32, (ATT_VD, Q_SUB), 0)
    lam = lam_ref[0][0:1, :]
    sg = sg_ref[...]

    def rows_of(t):
        if isinstance(t, int):
            return pl.ds(t * Q_SUB, Q_SUB)
        return pl.ds(pl.multiple_of(t * Q_SUB, Q_SUB), Q_SUB)

    def scores(t, slot):
        qt = q_ref[rows_of(t), :].astype(F32).T
        qbd = jnp.concatenate([jnp.where(row < ATT_HD, qt, 0.0),
                               jnp.where(row >= ATT_HD, qt, 0.0)], axis=1).astype(BF16)
        m8 = None
        for k_ref, off, n in k_parts:
            for c in range(n // KV_CHUNK):
                kch = k_ref[c * KV_CHUNK:(c + 1) * KV_CHUNK, :]
                st = jnp.dot(kch, qbd, preferred_element_type=F32)
                st_ref[slot][off + c * KV_CHUNK:off + (c + 1) * KV_CHUNK, :] = st
                for r in range(KV_CHUNK // 8):
                    piece = st[r * 8:(r + 1) * 8, :]
                    m8 = piece if m8 is None else jnp.maximum(m8, piece)
        return jnp.max(m8, axis=0, keepdims=True)

    def softmax(slot, m):
        l8 = None
        for c in range(lk // KV_CHUNK):
            rows = slice(c * KV_CHUNK, (c + 1) * KV_CHUNK)
            p = jnp.exp2(st_ref[slot][rows, :] - m)
            for r in range(KV_CHUNK // 8):
                piece = p[r * 8:(r + 1) * 8, :]
                l8 = piece if l8 is None else l8 + piece
            pt_ref[slot][rows, :] = p.astype(BF16)
        return l8

    def pv(t, slot, l8):
        acc = jnp.dot(vt_ref[...], pt_ref[slot][...], preferred_element_type=F32)
        o = acc * (1.0 / jnp.sum(l8, axis=0, keepdims=True))
        ot = o[:, :Q_SUB] - lam * o[:, Q_SUB:]
        oq = ot.T
        ms = jnp.mean(oq * oq, axis=-1, keepdims=True)
        y = oq * lax.rsqrt(ms + EPS) * sg * out_scale
        gate = g_ref[rows_of(t), :].astype(F32)
        out_ref[rows_of(t), :] = (y * _silu(gate)).astype(out_ref.dtype)

    def tick(t, slot, carry, do_scores=True, do_softmax=True, do_pv=True):
        m_prev, l8_prev = carry
        m = scores(t, slot) if do_scores else None
        l8 = softmax(1 - slot, m_prev) if do_softmax else None
        if do_pv:
            pv(t - 2, slot, l8_prev)
        return m, l8

    n_trips = (n_sub - 2) // TICKS_PER_TRIP if n_sub - 2 >= 2 * TICKS_PER_TRIP else 0
    n_loop = n_trips * TICKS_PER_TRIP
    carry = (None, None)
    t = 0
    while t < 2:
        carry = tick(t, t % 2, carry, do_scores=t < n_sub, do_softmax=1 <= t <= n_sub, do_pv=False)
        t += 1
    if n_loop:
        def trip(j, c):
            t0 = 2 + TICKS_PER_TRIP * j
            for u in range(TICKS_PER_TRIP):
                c = tick(t0 + u, u % 2, c)
            return c
        carry = lax.fori_loop(0, n_trips, trip, carry)
        t += n_loop
    while t < n_sub + 2:
        carry = tick(t, t % 2, carry, do_scores=t < n_sub, do_softmax=1 <= t <= n_sub, do_pv=True)
        t += 1


def _attention(q_arr, kv_ctx, kv_lat, lam, subln_g, *, batch, lq, tq, q_col, g_col, lc, kc_col, vc_col,
               ll, kl_col, vl_col, out_scale, name):
    nq = lq // tq
    hb = ATT_HEADS
    in_specs = [pl.BlockSpec((tq, LANES), lambda b, h, i: (b * nq + i, q_col + h)),
                pl.BlockSpec((lc, LANES), lambda b, h, i: (b, kc_col + h)),
                pl.BlockSpec((lc, LANES), lambda b, h, i: (b, vc_col + h))]
    args = [q_arr, kv_ctx, kv_ctx]
    if kv_lat is not None:
        in_specs += [pl.BlockSpec((ll, LANES), lambda b, h, i: (b, kl_col + h)),
                     pl.BlockSpec((ll, LANES), lambda b, h, i: (b, vl_col + h))]
        args += [kv_lat, kv_lat]
    else:
        ll = 0
    in_specs += [pl.BlockSpec((tq, LANES), lambda b, h, i: (b * nq + i, g_col + h)),
                 pl.BlockSpec((1, 8, LANES), lambda b, h, i: (0, 0, 0)),
                 pl.BlockSpec((1, ATT_VD), lambda b, h, i: (0, 0))]
    args += [q_arr, lam, subln_g]
    kern = functools.partial(_attn_kernel, lc=lc, ll=ll, n_sub=tq // Q_SUB, out_scale=out_scale)
    return pl.pallas_call(
        kern,
        grid=(batch, hb, nq),
        in_specs=in_specs,
        out_specs=pl.BlockSpec((tq, LANES), lambda b, h, i: (b * nq + i, h)),
        out_shape=jax.ShapeDtypeStruct((batch * lq, hb * ATT_VD), BF16),
        scratch_shapes=[pltpu.VMEM((ATT_VD, lc + ll), BF16),
                        pltpu.VMEM((lc + ll, 2 * Q_SUB), F32), pltpu.VMEM((lc + ll, 2 * Q_SUB), F32),
                        pltpu.VMEM((lc + ll, 2 * Q_SUB), BF16), pltpu.VMEM((lc + ll, 2 * Q_SUB), BF16)],
        compiler_params=_cparams(3),
        name=name,
    )(*args)


def _poolconv_kernel(up_c, up_p, up_n, gp_ref, a_c, a_p, a_n, b_c, b_p, b_n, gc_ref,
                     wpool_ref, pscale_ref, wdw_ref, bdw_ref, lng_ref, lnb_ref, wpw_ref,
                     ypool_ref, yconv_ref, u_ref, p_ref, z_ref, s_ref, r_ref, *, rows, tiles_per_seq, seq_len,
                     row_blk):
    iseq = pl.program_id(0) % tiles_per_seq
    has_prev = iseq > 0
    has_next = iseq < tiles_per_seq - 1

    def glu(a_ref, b_ref):
        return a_ref[...].astype(F32) * jax.nn.sigmoid(b_ref[...].astype(F32))

    u_ref[0:HALO, :] = jnp.where(has_prev, glu(a_p, b_p), 0.0)
    u_ref[HALO:HALO + rows, :] = glu(a_c, b_c)
    u_ref[HALO + rows:, :] = jnp.where(has_next, glu(a_n, b_n), 0.0)
    p_ref[0:HALO, :] = jnp.where(has_prev, up_p[...].astype(F32), 0.0)
    p_ref[HALO:HALO + rows, :] = up_c[...].astype(F32)
    p_ref[HALO + rows:, :] = jnp.where(has_next, up_n[...].astype(F32), 0.0)

    ext = rows + 2 * HALO - 8
    for b in range(1, 8):
        s_ref[b - 1] = u_ref[b:b + ext, :]

    for rb in range(rows // row_blk):
        acc = None
        for j in range(CONV_K):
            off = HALO - CONV_K // 2 + j
            base = rb * row_blk + (off // 8) * 8
            if off % 8 == 0:
                tap = u_ref[base:base + row_blk, :]
            else:
                tap = s_ref[off % 8 - 1, base:base + row_blk, :]
            term = tap * wdw_ref[j:j + 1, :]
            acc = term if acc is None else acc + term
        y = acc + bdw_ref[...]
        mu = jnp.mean(y, axis=-1, keepdims=True)
        yc = y - mu
        var = jnp.mean(yc * yc, axis=-1, keepdims=True)
        z = yc * lax.rsqrt(var + EPS) * lng_ref[...] + lnb_ref[...]
        z_ref[rb * row_blk:(rb + 1) * row_blk, :] = _silu(z).astype(BF16)
    yconv = jnp.dot(z_ref[...], wpw_ref[...], preferred_element_type=F32)
    yconv_ref[...] = (yconv * _silu(gc_ref[...].astype(F32))).astype(yconv_ref.dtype)

    t = iseq * rows + lax.broadcasted_iota(jnp.int32, (rows, POOL_GD), 0)
    for g, w in enumerate(POOL_WINDOWS):
        cols = slice(g * POOL_GD, (g + 1) * POOL_GD)
        hw = w // 2
        load = lambda s, m, _c=cols: p_ref[s:s + m, _c]
        n, level = 1, 0
        while n < hw:
            dst = r_ref.at[level % 2]
            dst[8:rows + HALO + 8, :] = load(8, rows + HALO) + load(8 + n, rows + HALO)
            dst[rows + HALO + 8:, :] = jnp.zeros((8, POOL_GD), F32)
            load = lambda s, m, _d=dst: _d[s:s + m, :]
            n, level = 2 * n, level + 1
        ssum = load(HALO - hw, rows) + load(HALO, rows)
        cnt = (jnp.minimum(t + hw, seq_len) - jnp.maximum(t - hw, 0)).astype(F32)
        dlt = ssum / cnt - p_ref[HALO:HALO + rows, cols]
        yp = jnp.dot(dlt.astype(BF16), wpool_ref[g], preferred_element_type=F32) * pscale_ref[:, cols]
        ypool_ref[:, cols] = (yp * _silu(gp_ref[:, cols].astype(F32))).astype(ypool_ref.dtype)


def _poolconv(proj, wpool, pscale, wdw, bdw, lng, lnb, wpw, *, seq_len, rows, col0, width, name):
    m_rows = proj.shape[0]
    tiles_per_seq = seq_len // rows
    hpr = rows // HALO
    n_halo = m_rows // HALO
    cb = col0 // width

    def cur(k):
        return pl.BlockSpec((rows, width), lambda i: (i, cb + k))

    def prev(k):
        return pl.BlockSpec((HALO, width), lambda i: (jnp.maximum(i * hpr - 1, 0), cb + k))

    def nxt(k):
        return pl.BlockSpec((HALO, width), lambda i: (jnp.minimum((i + 1) * hpr, n_halo - 1), cb + k))

    full = lambda a: pl.BlockSpec(a.shape, lambda i: (0,) * a.ndim)
    consts = [wpool, pscale, wdw, bdw, lng, lnb, wpw]
    kern = functools.partial(_poolconv_kernel, rows=rows, tiles_per_seq=tiles_per_seq, seq_len=seq_len,
                             row_blk=min(64, rows))
    return pl.pallas_call(
        kern,
        grid=(m_rows // rows,),
        in_specs=[cur(0), prev(0), nxt(0), cur(1), cur(2), prev(2), nxt(2), cur(3), prev(3), nxt(3), cur(4)]
                 + [full(a) for a in consts],
        out_specs=[pl.BlockSpec((rows, width), lambda i: (i, 0)),
                   pl.BlockSpec((rows, width), lambda i: (i, 0))],
        out_shape=[jax.ShapeDtypeStruct((m_rows, width), BF16),
                   jax.ShapeDtypeStruct((m_rows, width), BF16)],
        scratch_shapes=[pltpu.VMEM((rows + 2 * HALO, width), F32),
                        pltpu.VMEM((rows + 2 * HALO, width), F32),
                        pltpu.VMEM((rows, width), BF16),
                        pltpu.VMEM((7, rows + 2 * HALO - 8, width), F32),
                        pltpu.VMEM((2, rows + 2 * HALO, POOL_GD), F32)],
        compiler_params=_cparams(1),
        name=name,
    )(*([proj] * 11), *consts)


def _outproj_kernel(*refs, d_model, att_width, pool_width, final):
    if final:
        ya_ref, yp_ref, yc_ref, w_ref, x_ref, mod_ref, fg_ref, out_ref = refs
    else:
        ya_ref, yp_ref, yc_ref, w_ref, x_ref, mod_ref, out_ref = refs
    c1 = att_width
    c2 = att_width + pool_width
    y = jnp.dot(ya_ref[...], w_ref[0:c1, :], preferred_element_type=F32)
    y = y + jnp.dot(yp_ref[...], w_ref[c1:c2, :], preferred_element_type=F32)
    y = y + jnp.dot(yc_ref[...], w_ref[c2:, :], preferred_element_type=F32)
    gate = mod_ref[0][:, 2 * d_model:]
    xn = x_ref[...] + gate * y
    if final:
        ms = jnp.mean(xn * xn, axis=-1, keepdims=True)
        xn = xn * lax.rsqrt(ms + EPS) * fg_ref[...]
    out_ref[...] = xn


def _outproj(ya, yp, yc, w, x2d, mod3, final_g, *, layer, tm, tiles_per_mod, mod_row0, name):
    m_rows, d = x2d.shape
    final = final_g is not None
    in_specs = [pl.BlockSpec((tm, ya.shape[1]), lambda i: (i, 0)),
                pl.BlockSpec((tm, yp.shape[1]), lambda i: (i, 0)),
                pl.BlockSpec((tm, yc.shape[1]), lambda i: (i, 0)),
                pl.BlockSpec((None,) + w.shape[1:], lambda i: (layer, 0, 0)),
                pl.BlockSpec((tm, d), lambda i: (i, 0)),
                pl.BlockSpec((1, 1, mod3.shape[-1]), lambda i: (mod_row0 + i // tiles_per_mod, 0, 0))]
    args = [ya, yp, yc, w, x2d, mod3]
    if final:
        in_specs.append(pl.BlockSpec((1, d), lambda i: (0, 0)))
        args.append(final_g)
    kern = functools.partial(_outproj_kernel, d_model=d, att_width=ya.shape[1], pool_width=yp.shape[1],
                             final=final)
    return pl.pallas_call(
        kern,
        grid=(m_rows // tm,),
        in_specs=in_specs,
        out_specs=pl.BlockSpec((tm, d), lambda i: (i, 0)),
        out_shape=jax.ShapeDtypeStruct((m_rows, d), F32),
        compiler_params=_cparams(1),
        name=name,
    )(*args)


def _rope_tables(seq_len):
    n_freq = ATT_HD // 4
    inv_freq = (np.float32(ROPE_BASE) ** (-np.arange(n_freq, dtype=np.float32) / n_freq)).astype(np.float32)
    t = np.arange(seq_len)
    lane = np.arange(LANES)
    in_col_half = (lane % ATT_HD) >= ATT_HD // 2
    pos = np.where(in_col_half[None, :], (t % GRID_W)[:, None], (t // GRID_W)[:, None]).astype(np.float32)
    ang = pos * inv_freq[lane % n_freq][None, :]
    cos = np.cos(ang).astype(np.float32)
    sin = np.sin(ang).astype(np.float32)
    qs = np.float32(ATT_HD ** -0.5 * math.log2(math.e))
    perm = np.zeros((LANES, LANES), np.float32)
    for p in range(LANES):
        if p % (2 * n_freq) < n_freq:
            perm[p + n_freq, p] = -1.0
        else:
            perm[p - n_freq, p] = 1.0
    return (jnp.asarray(np.stack([cos * qs, cos])), jnp.asarray(np.stack([sin * qs, sin])),
            jnp.asarray(perm, dtype=BF16))


def kernel(x, c, ctx, c_ctx, w_mod, b_mod, norm_g, w_in, lambda_q1, lambda_k1, lambda_q2, lambda_k2,
           subln_g, w_pool, pool_scale, w_dw, b_dw, conv_ln_g, conv_ln_b, w_pw2, w_out, final_g):
    batch, seq, d = x.shape
    lc = ctx.shape[1]
    depth = w_mod.shape[0]
    att_w = d // 2
    pool_w = d // 4
    assert batch < 8 and seq % GRID_W == 0 and att_w == ATT_HEADS * ATT_VD and pool_w == 4 * POOL_GD
    assert seq % KV_CHUNK == 0 and lc % KV_CHUNK == 0

    cc = jnp.concatenate([c, c_ctx[None, :], jnp.zeros((8 - batch - 1, d), F32)], axis=0)
    lam_init = [0.8 - 0.6 * math.exp(-0.3 * l) for l in range(depth)]
    li = jnp.broadcast_to(jnp.asarray(lam_init, F32)[:, None, None], (depth, 1, LANES))
    mod, lam = _modulation(cc, w_mod, b_mod, lambda_q1, lambda_k1, lambda_q2, lambda_k2, li)
    tabs = _rope_tables(seq)
    w_in_b = w_in.astype(BF16)
    w_out_b = w_out.astype(BF16)

    tm_lat = min(1024, seq)
    tn = 512
    n_in = w_in.shape[-1]
    hcol = att_w // LANES
    pc_col0 = 4 * att_w
    x2d = x.reshape(batch * seq, d)
    c2d = ctx.reshape(batch * lc, d)
    row = lambda a: a.reshape(1, -1)

    for l in range(depth):
        last = l == depth - 1
        mod3 = mod[l].reshape(8, 1, 3 * d)
        lw = (w_pool[l].astype(BF16), row(pool_scale[l]), w_dw[l], row(b_dw[l]), row(conv_ln_g[l]),
              row(conv_ln_b[l]), w_pw2[l].astype(BF16))
        g_l = row(norm_g[l])
        out_scale = 1.0 - lam_init[l]

        proj = _inproj(x2d, mod3, g_l, w_in_b, tabs, layer=l, tm=tm_lat, tn=tn, tiles_per_mod=seq // tm_lat,
                       mod_row0=0, n0=0, n_tiles=n_in // tn, att_width=att_w, name=f"inproj_lat{l}")
        if last:
            cproj = _inproj(c2d, mod3, g_l, w_in_b, None, layer=l, tm=batch * lc, tn=tn, tiles_per_mod=1,
                            mod_row0=batch, n0=att_w // tn, n_tiles=2 * att_w // tn, att_width=att_w,
                            name=f"inproj_ctx{l}")
            kc_col, vc_col = 0, hcol
        else:
            cproj = _inproj(c2d, mod3, g_l, w_in_b, None, layer=l, tm=batch * lc, tn=tn, tiles_per_mod=1,
                            mod_row0=batch, n0=0, n_tiles=n_in // tn, att_width=att_w,
                            name=f"inproj_ctx{l}")
            kc_col, vc_col = hcol, 2 * hcol

        y_att = _attention(proj, cproj, proj, lam[l:l + 1], row(subln_g[l]), batch=batch, lq=seq,
                           tq=seq, q_col=0, g_col=3 * hcol, lc=lc, kc_col=kc_col, vc_col=vc_col,
                           ll=seq, kl_col=hcol, vl_col=2 * hcol, out_scale=out_scale, name=f"attn_lat{l}")
        y_pool, y_conv = _poolconv(proj, *lw, seq_len=seq, rows=min(256, seq), col0=pc_col0, width=pool_w,
                                   name=f"poolconv_lat{l}")
        if not last:
            yc_att = _attention(cproj, cproj, None, lam[l:l + 1], row(subln_g[l]), batch=batch, lq=lc,
                                tq=lc, q_col=0, g_col=3 * hcol, lc=lc, kc_col=kc_col, vc_col=vc_col,
                                ll=0, kl_col=0, vl_col=0, out_scale=out_scale, name=f"attn_ctx{l}")
            yc_pool, yc_conv = _poolconv(cproj, *lw, seq_len=lc, rows=min(256, lc), col0=pc_col0,
                                         width=pool_w, name=f"poolconv_ctx{l}")
            c2d = _outproj(yc_att, yc_pool, yc_conv, w_out_b, c2d, mod3, None, layer=l, tm=min(512, batch * lc),
                           tiles_per_mod=batch * lc, mod_row0=batch, name=f"outproj_ctx{l}")
        tm_o = min(512, seq)
        x2d = _outproj(y_att, y_pool, y_conv, w_out_b, x2d, mod3, row(final_g) if last else None,
                       layer=l, tm=tm_o, tiles_per_mod=seq // tm_o, mod_row0=0, name=f"outproj_lat{l}")
    return x2d.reshape(batch, seq, d)
```

```python
import functools
import math

import jax
import jax.numpy as jnp
import numpy as np
from jax import lax
from jax.experimental import pallas as pl
from jax.experimental.pallas import tpu as pltpu

ATT_HEADS = 8
ATT_HD = 64
ATT_VD = 2 * ATT_HD
POOL_WINDOWS = (2, 4, 8, 16)
POOL_GD = 128
CONV_K = 31
GRID_W = 64
ROPE_BASE = 10000.0
EPS = 1e-6

LANES = 128
KV_CHUNK = 256
Q_SUB = 128
SM_GROUP = 32
SM_AHEAD = 2
TICKS_PER_TRIP = 14
HALO = 16
VMEM_LIMIT = 56 * 1024 * 1024

F32 = jnp.float32
BF16 = jnp.bfloat16


def _cparams(n_axes, flags=None):
    return pltpu.CompilerParams(dimension_semantics=("arbitrary",) * n_axes,
                                vmem_limit_bytes=VMEM_LIMIT, flags=flags)


def _silu(v):
    return v * jax.nn.sigmoid(v)


def _mod_kernel(cc_ref, w_ref, b_ref, lq1_ref, lk1_ref, lq2_ref, lk2_ref, li_ref, out_ref, lam_ref):
    s = _silu(cc_ref[...])
    out_ref[0] = jnp.dot(s.astype(BF16), w_ref[0].astype(BF16), preferred_element_type=F32) + b_ref[0]
    a1 = jnp.sum(lq1_ref[0] * lk1_ref[0], axis=-1, keepdims=True)
    a2 = jnp.sum(lq2_ref[0] * lk2_ref[0], axis=-1, keepdims=True)
    lam = jnp.exp(a1) - jnp.exp(a2) + li_ref[0]
    lam_ref[0] = jnp.broadcast_to(lam, lam_ref.shape[1:])


def _modulation(cc, w_mod, b_mod, lq1, lk1, lq2, lk2, lam_init):
    depth, d, n = w_mod.shape
    tn = n // 4 if n % (4 * LANES) == 0 else n
    vec = lambda a: a.reshape(depth, 1, a.shape[-1])
    lspec = pl.BlockSpec((1, 1, ATT_HD), lambda l, j: (l, 0, 0))
    return pl.pallas_call(
        _mod_kernel,
        grid=(depth, n // tn),
        in_specs=[pl.BlockSpec((8, d), lambda l, j: (0, 0)),
                  pl.BlockSpec((1, d, tn), lambda l, j: (l, 0, j)),
                  pl.BlockSpec((1, 1, tn), lambda l, j: (l, 0, j)),
                  lspec, lspec, lspec, lspec,
                  pl.BlockSpec((1, 1, LANES), lambda l, j: (l, 0, 0))],
        out_specs=[pl.BlockSpec((1, 8, tn), lambda l, j: (l, 0, j)),
                   pl.BlockSpec((1, 8, LANES), lambda l, j: (l, 0, 0))],
        out_shape=[jax.ShapeDtypeStruct((depth, 8, n), F32),
                   jax.ShapeDtypeStruct((depth, 8, LANES), F32)],
        compiler_params=_cparams(2),
        name="modulation",
    )(cc, w_mod, vec(b_mod), vec(lq1), vec(lk1), vec(lq2), vec(lk2), lam_init)


def _inproj_kernel(*refs, d_model, rope, n_rope_tiles, n_q_tiles, row_chunk):
    if rope:
        x_ref, mod_ref, g_ref, w_ref, cos_ref, sin_ref, perm_ref, out_ref, h_ref = refs
    else:
        x_ref, mod_ref, g_ref, w_ref, out_ref, h_ref = refs
    j = pl.program_id(1)
    tm, tn = out_ref.shape

    def rope_store(rows, acc):
        cos = cos_ref[rows, :]
        sin = sin_ref[rows, :]
        for hh in range(tn // LANES):
            t = acc[:, hh * LANES:(hh + 1) * LANES]
            rot = jnp.dot(t.astype(BF16), perm_ref[...], preferred_element_type=F32)
            out_ref[rows, hh * LANES:(hh + 1) * LANES] = (t * cos + rot * sin).astype(out_ref.dtype)

    def plain_store(rows, acc):
        qs = jnp.where(j < n_q_tiles, ATT_HD ** -0.5 * math.log2(math.e), 1.0).astype(F32)
        out_ref[rows, :] = (acc * qs).astype(out_ref.dtype)

    @pl.when(j == 0)
    def _():
        m = mod_ref[0]
        shift = m[:, :d_model]
        gain = g_ref[...] * (1.0 + m[:, d_model:2 * d_model])
        for c in range(tm // row_chunk):
            rows = slice(c * row_chunk, (c + 1) * row_chunk)
            xf = x_ref[rows, :]
            ms = jnp.mean(xf * xf, axis=-1, keepdims=True)
            h = (xf * lax.rsqrt(ms + EPS) * gain + shift).astype(BF16)
            h_ref[rows, :] = h
            acc = jnp.dot(h, w_ref[...], preferred_element_type=F32)
            if rope:
                rope_store(rows, acc)
            else:
                plain_store(rows, acc)

    if rope and n_rope_tiles > 1:
        @pl.when(jnp.logical_and(j > 0, j < n_rope_tiles))
        def _():
            rope_store(slice(None), jnp.dot(h_ref[...], w_ref[...], preferred_element_type=F32))

    @pl.when(j >= (n_rope_tiles if rope else 1))
    def _():
        plain_store(slice(None), jnp.dot(h_ref[...], w_ref[...], preferred_element_type=F32))


def _inproj(x2d, mod3, g, w, tabs, *, layer, tm, tn, tiles_per_mod, mod_row0, n0, n_tiles, att_width, name):
    m_rows, d = x2d.shape
    rope = tabs is not None
    n_q_tiles = max(att_width // tn - n0, 0)
    in_specs = [pl.BlockSpec((tm, d), lambda i, j: (i, 0)),
                pl.BlockSpec((1, 1, mod3.shape[-1]), lambda i, j: (mod_row0 + i // tiles_per_mod, 0, 0)),
                pl.BlockSpec((1, d), lambda i, j: (0, 0)),
                pl.BlockSpec((None, d, tn), lambda i, j: (layer, 0, n0 + j))]
    args = [x2d, mod3, g, w]
    if rope:
        tiles_per_seq = tabs[0].shape[1] // tm
        tspec = pl.BlockSpec((None, tm, LANES),
                             lambda i, j: (jnp.where(j < n_q_tiles, 0, 1), i % tiles_per_seq, 0))
        in_specs += [tspec, tspec, pl.BlockSpec((LANES, LANES), lambda i, j: (0, 0))]
        args += list(tabs)
    kern = functools.partial(_inproj_kernel, d_model=d, rope=rope,
                             n_rope_tiles=max(2 * att_width // tn - n0, 0), n_q_tiles=n_q_tiles,
                             row_chunk=min(256, tm))
    return pl.pallas_call(
        kern,
        grid=(m_rows // tm, n_tiles),
        in_specs=in_specs,
        out_specs=pl.BlockSpec((tm, tn), lambda i, j: (i, j)),
        out_shape=jax.ShapeDtypeStruct((m_rows, n_tiles * tn), BF16),
        scratch_shapes=[pltpu.VMEM((tm, d), BF16)],
        compiler_params=_cparams(2),
        name=name,
    )(*args)


def _attn_kernel(*refs, lc, ll, n_sub, heads, out_scale):
    if ll:
        q_ref, kc_ref, vc_ref, kl_ref, vl_ref, g_ref, lam_ref, sg_ref, out_ref, vt_ref, *slots = refs
        k_parts = [(kc_ref, 0, lc), (kl_ref, lc, ll)]
        v_parts = [(vc_ref, 0, lc), (vl_ref, lc, ll)]
    else:
        q_ref, kc_ref, vc_ref, g_ref, lam_ref, sg_ref, out_ref, vt_ref, *slots = refs
        k_parts = [(kc_ref, 0, lc)]
        v_parts = [(vc_ref, 0, lc)]
    lk = lc + ll
    st_ref = slots[0:2]
    pt_ref = slots[2:4]

    def head_lanes(hh):
        return slice(hh * LANES, (hh + 1) * LANES)

    @pl.when(pl.program_id(2) == 0)
    def _():
        for hh in range(heads):
            for ref, off, n in v_parts:
                for c in range(n // KV_CHUNK):
                    blk = ref[c * KV_CHUNK:(c + 1) * KV_CHUNK, head_lanes(hh)].astype(F32)
                    vt_ref[hh, :, off + c * KV_CHUNK:off + (c + 1) * KV_CHUNK] = blk.T.astype(BF16)

    row = lax.broadcasted_iota(jnp.int32, (ATT_VD, Q_SUB), 0)
    lam = lam_ref[0][0:1, :]
    sg = sg_ref[...]

    def item(t):
        if isinstance(t, int):
            return t // n_sub, pl.ds((t % n_sub) * Q_SUB, Q_SUB)
        assert heads == 1
        return 0, pl.ds(pl.multiple_of(t * Q_SUB, Q_SUB), Q_SUB)

    def scores(t, slot):
        hh, q_rows = item(t)
        qt = q_ref[q_rows, head_lanes(hh)].astype(F32).T
        qbd = jnp.concatenate([jnp.where(row < ATT_HD, qt, 0.0),
                               jnp.where(row >= ATT_HD, qt, 0.0)], axis=1).astype(BF16)
        m8 = None
        for k_ref, off, n in k_parts:
            for c in range(n // KV_CHUNK):
                kch = k_ref[c * KV_CHUNK:(c + 1) * KV_CHUNK, head_lanes(hh)]
                st = jnp.dot(kch, qbd, preferred_element_type=F32)
                st_ref[slot][off + c * KV_CHUNK:off + (c + 1) * KV_CHUNK, :] = st
                for r in range(KV_CHUNK // 8):
                    piece = st[r * 8:(r + 1) * 8, :]
                    m8 = piece if m8 is None else jnp.maximum(m8, piece)
        return jnp.max(m8, axis=0, keepdims=True)

    def softmax(slot, m):
        l8 = None
        marks = []
        m8 = jnp.broadcast_to(m, (8, 2 * Q_SUB))
        for g in range(lk // SM_GROUP):
            rows = slice(g * SM_GROUP, (g + 1) * SM_GROUP)
            m_use = m8 if g < SM_AHEAD else m8 + marks[g - SM_AHEAD] * 0.0
            p = jnp.exp2(st_ref[slot][rows, :] - jnp.tile(m_use, (SM_GROUP // 8, 1)))
            for r in range(SM_GROUP // 8):
                piece = p[r * 8:(r + 1) * 8, :]
                l8 = piece if l8 is None else l8 + piece
            marks.append(l8)
            pt_ref[slot][rows, :] = p.astype(BF16)
        return l8

    def pv(t, slot, l8):
        hh, q_rows = item(t)
        acc = jnp.dot(vt_ref[hh], pt_ref[slot][...], preferred_element_type=F32)
        o = acc * (1.0 / jnp.sum(l8, axis=0, keepdims=True))
        ot = o[:, :Q_SUB] - lam * o[:, Q_SUB:]
        oq = ot.T
        ms = jnp.mean(oq * oq, axis=-1, keepdims=True)
        y = oq * lax.rsqrt(ms + EPS) * sg * out_scale
        gate = g_ref[q_rows, head_lanes(hh)].astype(F32)
        out_ref[q_rows, head_lanes(hh)] = (y * _silu(gate)).astype(out_ref.dtype)

    def tick(t, slot, carry, do_scores=True, do_softmax=True, do_pv=True):
        m_prev, l8_prev = carry
        m = scores(t, slot) if do_scores else None
        l8 = softmax(1 - slot, m_prev) if do_softmax else None
        if do_pv:
            pv(t - 2, slot, l8_prev)
        return m, l8

    n_items = heads * n_sub
    n_trips = (n_items - 2) // TICKS_PER_TRIP if heads == 1 and n_items - 2 >= 2 * TICKS_PER_TRIP else 0
    n_loop = n_trips * TICKS_PER_TRIP
    carry = (None, None)
    t = 0
    while t < 2:
        carry = tick(t, t % 2, carry, do_scores=t < n_items, do_softmax=1 <= t <= n_items, do_pv=False)
        t += 1
    if n_loop:
        def trip(j, c):
            t0 = 2 + TICKS_PER_TRIP * j
            for u in range(TICKS_PER_TRIP):
                c = tick(t0 + u, u % 2, c)
            return c
        carry = lax.fori_loop(0, n_trips, trip, carry)
        t += n_loop
    while t < n_items + 2:
        carry = tick(t, t % 2, carry, do_scores=t < n_items, do_softmax=1 <= t <= n_items, do_pv=True)
        t += 1


def _attention(q_arr, kv_ctx, kv_lat, lam, subln_g, *, batch, lq, tq, heads, q_col, g_col, lc, kc_col, vc_col,
               ll, kl_col, vl_col, out_scale, name):
    nq = lq // tq
    width = heads * LANES
    assert ATT_HEADS % heads == 0 and all(c % heads == 0 for c in (q_col, g_col, kc_col, vc_col, kl_col, vl_col))
    col = lambda c: (lambda b, h, i: (b, c // heads + h))
    qcol = lambda c: (lambda b, h, i: (b * nq + i, c // heads + h))
    in_specs = [pl.BlockSpec((tq, width), qcol(q_col)),
                pl.BlockSpec((lc, width), col(kc_col)),
                pl.BlockSpec((lc, width), col(vc_col))]
    args = [q_arr, kv_ctx, kv_ctx]
    if kv_lat is not None:
        in_specs += [pl.BlockSpec((ll, width), col(kl_col)),
                     pl.BlockSpec((ll, width), col(vl_col))]
        args += [kv_lat, kv_lat]
    else:
        ll = 0
    in_specs += [pl.BlockSpec((tq, width), qcol(g_col)),
                 pl.BlockSpec((1, 8, LANES), lambda b, h, i: (0, 0, 0)),
                 pl.BlockSpec((1, ATT_VD), lambda b, h, i: (0, 0))]
    args += [q_arr, lam, subln_g]
    kern = functools.partial(_attn_kernel, lc=lc, ll=ll, n_sub=tq // Q_SUB, heads=heads, out_scale=out_scale)
    return pl.pallas_call(
        kern,
        grid=(batch, ATT_HEADS // heads, nq),
        in_specs=in_specs,
        out_specs=pl.BlockSpec((tq, width), lambda b, h, i: (b * nq + i, h)),
        out_shape=jax.ShapeDtypeStruct((batch * lq, ATT_HEADS * ATT_VD), BF16),
        scratch_shapes=[pltpu.VMEM((heads, ATT_VD, lc + ll), BF16),
                        pltpu.VMEM((lc + ll, 2 * Q_SUB), F32), pltpu.VMEM((lc + ll, 2 * Q_SUB), F32),
                        pltpu.VMEM((lc + ll, 2 * Q_SUB), BF16), pltpu.VMEM((lc + ll, 2 * Q_SUB), BF16)],
        compiler_params=_cparams(3),
        name=name,
    )(*args)


def _poolconv_kernel(up_c, up_p, up_n, gp_ref, a_c, a_p, a_n, b_c, b_p, b_n, gc_ref,
                     wpool_ref, pscale_ref, wdw_ref, bdw_ref, lng_ref, lnb_ref, wpw_ref,
                     ypool_ref, yconv_ref, u_ref, p_ref, z_ref, s_ref, r_ref, *, rows, tiles_per_seq, seq_len,
                     row_blk):
    iseq = pl.program_id(0) % tiles_per_seq
    has_prev = iseq > 0
    has_next = iseq < tiles_per_seq - 1

    def glu(a_ref, b_ref):
        return a_ref[...].astype(F32) * jax.nn.sigmoid(b_ref[...].astype(F32))

    u_ref[0:HALO, :] = jnp.where(has_prev, glu(a_p, b_p), 0.0)
    u_ref[HALO:HALO + rows, :] = glu(a_c, b_c)
    u_ref[HALO + rows:, :] = jnp.where(has_next, glu(a_n, b_n), 0.0)
    p_ref[0:HALO, :] = jnp.where(has_prev, up_p[...].astype(F32), 0.0)
    p_ref[HALO:HALO + rows, :] = up_c[...].astype(F32)
    p_ref[HALO + rows:, :] = jnp.where(has_next, up_n[...].astype(F32), 0.0)

    ext = rows + 2 * HALO - 8
    for b in range(1, 8):
        s_ref[b - 1] = u_ref[b:b + ext, :]

    for rb in range(rows // row_blk):
        acc = None
        for j in range(CONV_K):
            off = HALO - CONV_K // 2 + j
            base = rb * row_blk + (off // 8) * 8
            if off % 8 == 0:
                tap = u_ref[base:base + row_blk, :]
            else:
                tap = s_ref[off % 8 - 1, base:base + row_blk, :]
            term = tap * wdw_ref[j:j + 1, :]
            acc = term if acc is None else acc + term
        y = acc + bdw_ref[...]
        mu = jnp.mean(y, axis=-1, keepdims=True)
        yc = y - mu
        var = jnp.mean(yc * yc, axis=-1, keepdims=True)
        z = yc * lax.rsqrt(var + EPS) * lng_ref[...] + lnb_ref[...]
        z_ref[rb * row_blk:(rb + 1) * row_blk, :] = _silu(z).astype(BF16)
    yconv = jnp.dot(z_ref[...], wpw_ref[...], preferred_element_type=F32)
    yconv_ref[...] = (yconv * _silu(gc_ref[...].astype(F32))).astype(yconv_ref.dtype)

    t = iseq * rows + lax.broadcasted_iota(jnp.int32, (rows, POOL_GD), 0)
    for g, w in enumerate(POOL_WINDOWS):
        cols = slice(g * POOL_GD, (g + 1) * POOL_GD)
        hw = w // 2
        load = lambda s, m, _c=cols: p_ref[s:s + m, _c]
        n, level = 1, 0
        while n < hw:
            dst = r_ref.at[level % 2]
            dst[8:rows + HALO + 8, :] = load(8, rows + HALO) + load(8 + n, rows + HALO)
            dst[rows + HALO + 8:, :] = jnp.zeros((8, POOL_GD), F32)
            load = lambda s, m, _d=dst: _d[s:s + m, :]
            n, level = 2 * n, level + 1
        ssum = load(HALO - hw, rows) + load(HALO, rows)
        cnt = (jnp.minimum(t + hw, seq_len) - jnp.maximum(t - hw, 0)).astype(F32)
        dlt = ssum / cnt - p_ref[HALO:HALO + rows, cols]
        yp = jnp.dot(dlt.astype(BF16), wpool_ref[g], preferred_element_type=F32) * pscale_ref[:, cols]
        ypool_ref[:, cols] = (yp * _silu(gp_ref[:, cols].astype(F32))).astype(ypool_ref.dtype)


def _poolconv(proj, wpool, pscale, wdw, bdw, lng, lnb, wpw, *, seq_len, rows, col0, width, name):
    m_rows = proj.shape[0]
    tiles_per_seq = seq_len // rows
    hpr = rows // HALO
    n_halo = m_rows // HALO
    cb = col0 // width

    def cur(k):
        return pl.BlockSpec((rows, width), lambda i: (i, cb + k))

    def prev(k):
        return pl.BlockSpec((HALO, width), lambda i: (jnp.maximum(i * hpr - 1, 0), cb + k))

    def nxt(k):
        return pl.BlockSpec((HALO, width), lambda i: (jnp.minimum((i + 1) * hpr, n_halo - 1), cb + k))

    full = lambda a: pl.BlockSpec(a.shape, lambda i: (0,) * a.ndim)
    consts = [wpool, pscale, wdw, bdw, lng, lnb, wpw]
    kern = functools.partial(_poolconv_kernel, rows=rows, tiles_per_seq=tiles_per_seq, seq_len=seq_len,
                             row_blk=min(64, rows))
    return pl.pallas_call(
        kern,
        grid=(m_rows // rows,),
        in_specs=[cur(0), prev(0), nxt(0), cur(1), cur(2), prev(2), nxt(2), cur(3), prev(3), nxt(3), cur(4)]
                 + [full(a) for a in consts],
        out_specs=[pl.BlockSpec((rows, width), lambda i: (i, 0)),
                   pl.BlockSpec((rows, width), lambda i: (i, 0))],
        out_shape=[jax.ShapeDtypeStruct((m_rows, width), BF16),
                   jax.ShapeDtypeStruct((m_rows, width), BF16)],
        scratch_shapes=[pltpu.VMEM((rows + 2 * HALO, width), F32),
                        pltpu.VMEM((rows + 2 * HALO, width), F32),
                        pltpu.VMEM((rows, width), BF16),
                        pltpu.VMEM((7, rows + 2 * HALO - 8, width), F32),
                        pltpu.VMEM((2, rows + 2 * HALO, POOL_GD), F32)],
        compiler_params=_cparams(1),
        name=name,
    )(*([proj] * 11), *consts)


def _outproj_kernel(*refs, d_model, att_width, pool_width, final):
    if final:
        ya_ref, yp_ref, yc_ref, w_ref, x_ref, mod_ref, fg_ref, out_ref = refs
    else:
        ya_ref, yp_ref, yc_ref, w_ref, x_ref, mod_ref, out_ref = refs
    c1 = att_width
    c2 = att_width + pool_width
    y = jnp.dot(ya_ref[...], w_ref[0:c1, :], preferred_element_type=F32)
    y = y + jnp.dot(yp_ref[...], w_ref[c1:c2, :], preferred_element_type=F32)
    y = y + jnp.dot(yc_ref[...], w_ref[c2:, :], preferred_element_type=F32)
    gate = mod_ref[0][:, 2 * d_model:]
    xn = x_ref[...] + gate * y
    if final:
        ms = jnp.mean(xn * xn, axis=-1, keepdims=True)
        xn = xn * lax.rsqrt(ms + EPS) * fg_ref[...]
    out_ref[...] = xn


def _outproj(ya, yp, yc, w, x2d, mod3, final_g, *, layer, tm, tiles_per_mod, mod_row0, name):
    m_rows, d = x2d.shape
    final = final_g is not None
    in_specs = [pl.BlockSpec((tm, ya.shape[1]), lambda i: (i, 0)),
                pl.BlockSpec((tm, yp.shape[1]), lambda i: (i, 0)),
                pl.BlockSpec((tm, yc.shape[1]), lambda i: (i, 0)),
                pl.BlockSpec((None,) + w.shape[1:], lambda i: (layer, 0, 0)),
                pl.BlockSpec((tm, d), lambda i: (i, 0)),
                pl.BlockSpec((1, 1, mod3.shape[-1]), lambda i: (mod_row0 + i // tiles_per_mod, 0, 0))]
    args = [ya, yp, yc, w, x2d, mod3]
    if final:
        in_specs.append(pl.BlockSpec((1, d), lambda i: (0, 0)))
        args.append(final_g)
    kern = functools.partial(_outproj_kernel, d_model=d, att_width=ya.shape[1], pool_width=yp.shape[1],
                             final=final)
    return pl.pallas_call(
        kern,
        grid=(m_rows // tm,),
        in_specs=in_specs,
        out_specs=pl.BlockSpec((tm, d), lambda i: (i, 0)),
        out_shape=jax.ShapeDtypeStruct((m_rows, d), F32),
        compiler_params=_cparams(1),
        name=name,
    )(*args)


def _rope_tables(seq_len):
    n_freq = ATT_HD // 4
    inv_freq = (np.float32(ROPE_BASE) ** (-np.arange(n_freq, dtype=np.float32) / n_freq)).astype(np.float32)
    t = np.arange(seq_len)
    lane = np.arange(LANES)
    in_col_half = (lane % ATT_HD) >= ATT_HD // 2
    pos = np.where(in_col_half[None, :], (t % GRID_W)[:, None], (t // GRID_W)[:, None]).astype(np.float32)
    ang = pos * inv_freq[lane % n_freq][None, :]
    cos = np.cos(ang).astype(np.float32)
    sin = np.sin(ang).astype(np.float32)
    qs = np.float32(ATT_HD ** -0.5 * math.log2(math.e))
    perm = np.zeros((LANES, LANES), np.float32)
    for p in range(LANES):
        if p % (2 * n_freq) < n_freq:
            perm[p + n_freq, p] = -1.0
        else:
            perm[p - n_freq, p] = 1.0
    return (jnp.asarray(np.stack([cos * qs, cos])), jnp.asarray(np.stack([sin * qs, sin])),
            jnp.asarray(perm, dtype=BF16))


def kernel(x, c, ctx, c_ctx, w_mod, b_mod, norm_g, w_in, lambda_q1, lambda_k1, lambda_q2, lambda_k2,
           subln_g, w_pool, pool_scale, w_dw, b_dw, conv_ln_g, conv_ln_b, w_pw2, w_out, final_g):
    batch, seq, d = x.shape
    lc = ctx.shape[1]
    depth = w_mod.shape[0]
    att_w = d // 2
    pool_w = d // 4
    assert batch < 8 and seq % GRID_W == 0 and att_w == ATT_HEADS * ATT_VD and pool_w == 4 * POOL_GD
    assert seq % KV_CHUNK == 0 and lc % KV_CHUNK == 0

    cc = jnp.concatenate([c, c_ctx[None, :], jnp.zeros((8 - batch - 1, d), F32)], axis=0)
    lam_init = [0.8 - 0.6 * math.exp(-0.3 * l) for l in range(depth)]
    li = jnp.broadcast_to(jnp.asarray(lam_init, F32)[:, None, None], (depth, 1, LANES))
    mod, lam = _modulation(cc, w_mod, b_mod, lambda_q1, lambda_k1, lambda_q2, lambda_k2, li)
    tabs = _rope_tables(seq)
    w_in_b = w_in.astype(BF16)
    w_out_b = w_out.astype(BF16)

    tm_lat = min(1024, seq)
    tn = 512
    n_in = w_in.shape[-1]
    hcol = att_w // LANES
    pc_col0 = 4 * att_w
    x2d = x.reshape(batch * seq, d)
    c2d = ctx.reshape(batch * lc, d)
    row = lambda a: a.reshape(1, -1)

    for l in range(depth):
        last = l == depth - 1
        mod3 = mod[l].reshape(8, 1, 3 * d)
        lw = (w_pool[l].astype(BF16), row(pool_scale[l]), w_dw[l], row(b_dw[l]), row(conv_ln_g[l]),
              row(conv_ln_b[l]), w_pw2[l].astype(BF16))
        g_l = row(norm_g[l])
        out_scale = 1.0 - lam_init[l]

        proj = _inproj(x2d, mod3, g_l, w_in_b, tabs, layer=l, tm=tm_lat, tn=tn, tiles_per_mod=seq // tm_lat,
                       mod_row0=0, n0=0, n_tiles=n_in // tn, att_width=att_w, name=f"inproj_lat{l}")
        if last:
            cproj = _inproj(c2d, mod3, g_l, w_in_b, None, layer=l, tm=batch * lc, tn=tn, tiles_per_mod=1,
                            mod_row0=batch, n0=att_w // tn, n_tiles=2 * att_w // tn, att_width=att_w,
                            name=f"inproj_ctx{l}")
            kc_col, vc_col = 0, hcol
        else:
            cproj = _inproj(c2d, mod3, g_l, w_in_b, None, layer=l, tm=batch * lc, tn=tn, tiles_per_mod=1,
                            mod_row0=batch, n0=0, n_tiles=n_in // tn, att_width=att_w,
                            name=f"inproj_ctx{l}")
            kc_col, vc_col = hcol, 2 * hcol

        y_att = _attention(proj, cproj, proj, lam[l:l + 1], row(subln_g[l]), batch=batch, lq=seq,
                           tq=seq, heads=1, q_col=0, g_col=3 * hcol, lc=lc, kc_col=kc_col, vc_col=vc_col,
                           ll=seq, kl_col=hcol, vl_col=2 * hcol, out_scale=out_scale, name=f"attn_lat{l}")
        y_pool, y_conv = _poolconv(proj, *lw, seq_len=seq, rows=min(512, seq), col0=pc_col0, width=pool_w,
                                   name=f"poolconv_lat{l}")
        if not last:
            yc_att = _attention(cproj, cproj, None, lam[l:l + 1], row(subln_g[l]), batch=batch, lq=lc,
                                tq=lc, heads=ATT_HEADS, q_col=0, g_col=3 * hcol, lc=lc, kc_col=kc_col, vc_col=vc_col,
                                ll=0, kl_col=0, vl_col=0, out_scale=out_scale, name=f"attn_ctx{l}")
            yc_pool, yc_conv = _poolconv(cproj, *lw, seq_len=lc, rows=min(256, lc), col0=pc_col0,
                                         width=pool_w, name=f"poolconv_ctx{l}")
            c2d = _outproj(yc_att, yc_pool, yc_conv, w_out_b, c2d, mod3, None, layer=l, tm=min(512, batch * lc),
                           tiles_per_mod=batch * lc, mod_row0=batch, name=f"outproj_ctx{l}")
        tm_o = min(512, seq)
        x2d = _outproj(y_att, y_pool, y_conv, w_out_b, x2d, mod3, row(final_g) if last else None,
                       layer=l, tm=tm_o, tiles_per_mod=seq // tm_o, mod_row0=0, name=f"outproj_lat{l}")
    return x2d.reshape(batch, seq, d)
```

```python
import functools
import math

import jax
import jax.numpy as jnp
import numpy as np
from jax import lax
from jax.experimental import pallas as pl
from jax.experimental.pallas import tpu as pltpu

ATT_HEADS = 8
ATT_HD = 64
ATT_VD = 2 * ATT_HD
POOL_WINDOWS = (2, 4, 8, 16)
POOL_GD = 128
CONV_K = 31
GRID_W = 64
ROPE_BASE = 10000.0
EPS = 1e-6

LANES = 128
KV_CHUNK = 256
Q_SUB = 128
SM_GROUP = 32
SM_AHEAD = 2
TICKS_PER_TRIP = 14
HALO = 16
VMEM_LIMIT = 56 * 1024 * 1024

F32 = jnp.float32
BF16 = jnp.bfloat16


def _cparams(n_axes, flags=None):
    return pltpu.CompilerParams(dimension_semantics=("arbitrary",) * n_axes,
                                vmem_limit_bytes=VMEM_LIMIT, flags=flags)


def _silu(v):
    return v * jax.nn.sigmoid(v)


def _mod_kernel(cc_ref, w_ref, b_ref, lq1_ref, lk1_ref, lq2_ref, lk2_ref, li_ref, out_ref, lam_ref):
    s = _silu(cc_ref[...])
    out_ref[0] = jnp.dot(s.astype(BF16), w_ref[0].astype(BF16), preferred_element_type=F32) + b_ref[0]
    a1 = jnp.sum(lq1_ref[0] * lk1_ref[0], axis=-1, keepdims=True)
    a2 = jnp.sum(lq2_ref[0] * lk2_ref[0], axis=-1, keepdims=True)
    lam = jnp.exp(a1) - jnp.exp(a2) + li_ref[0]
    lam_ref[0] = jnp.broadcast_to(lam, lam_ref.shape[1:])


def _modulation(cc, w_mod, b_mod, lq1, lk1, lq2, lk2, lam_init):
    depth, d, n = w_mod.shape
    tn = n // 4 if n % (4 * LANES) == 0 else n
    vec = lambda a: a.reshape(depth, 1, a.shape[-1])
    lspec = pl.BlockSpec((1, 1, ATT_HD), lambda l, j: (l, 0, 0))
    return pl.pallas_call(
        _mod_kernel,
        grid=(depth, n // tn),
        in_specs=[pl.BlockSpec((8, d), lambda l, j: (0, 0)),
                  pl.BlockSpec((1, d, tn), lambda l, j: (l, 0, j)),
                  pl.BlockSpec((1, 1, tn), lambda l, j: (l, 0, j)),
                  lspec, lspec, lspec, lspec,
                  pl.BlockSpec((1, 1, LANES), lambda l, j: (l, 0, 0))],
        out_specs=[pl.BlockSpec((1, 8, tn), lambda l, j: (l, 0, j)),
                   pl.BlockSpec((1, 8, LANES), lambda l, j: (l, 0, 0))],
        out_shape=[jax.ShapeDtypeStruct((depth, 8, n), F32),
                   jax.ShapeDtypeStruct((depth, 8, LANES), F32)],
        compiler_params=_cparams(2),
        name="modulation",
    )(cc, w_mod, vec(b_mod), vec(lq1), vec(lk1), vec(lq2), vec(lk2), lam_init)


def _inproj_kernel(*refs, d_model, rope, n_rope_tiles, n_q_tiles, row_chunk):
    if rope:
        x_ref, mod_ref, g_ref, w_ref, cos_ref, sin_ref, perm_ref, out_ref, h_ref = refs
    else:
        x_ref, mod_ref, g_ref, w_ref, out_ref, h_ref = refs
    j = pl.program_id(1)
    tm, tn = out_ref.shape

    def rope_store(rows, acc):
        cos = cos_ref[rows, :]
        sin = sin_ref[rows, :]
        for hh in range(tn // LANES):
            t = acc[:, hh * LANES:(hh + 1) * LANES]
            rot = jnp.dot(t.astype(BF16), perm_ref[...], preferred_element_type=F32)
            out_ref[rows, hh * LANES:(hh + 1) * LANES] = (t * cos + rot * sin).astype(out_ref.dtype)

    def plain_store(rows, acc):
        qs = jnp.where(j < n_q_tiles, ATT_HD ** -0.5 * math.log2(math.e), 1.0).astype(F32)
        out_ref[rows, :] = (acc * qs).astype(out_ref.dtype)

    @pl.when(j == 0)
    def _():
        m = mod_ref[0]
        shift = m[:, :d_model]
        gain = g_ref[...] * (1.0 + m[:, d_model:2 * d_model])
        wb = w_ref[...].astype(BF16)
        for c in range(tm // row_chunk):
            rows = slice(c * row_chunk, (c + 1) * row_chunk)
            xf = x_ref[rows, :]
            ms = jnp.mean(xf * xf, axis=-1, keepdims=True)
            h = (xf * lax.rsqrt(ms + EPS) * gain + shift).astype(BF16)
            h_ref[rows, :] = h
            acc = jnp.dot(h, wb, preferred_element_type=F32)
            if rope:
                rope_store(rows, acc)
            else:
                plain_store(rows, acc)

    if rope and n_rope_tiles > 1:
        @pl.when(jnp.logical_and(j > 0, j < n_rope_tiles))
        def _():
            rope_store(slice(None), jnp.dot(h_ref[...], w_ref[...].astype(BF16), preferred_element_type=F32))

    @pl.when(j >= (n_rope_tiles if rope else 1))
    def _():
        plain_store(slice(None), jnp.dot(h_ref[...], w_ref[...].astype(BF16), preferred_element_type=F32))


def _inproj(x2d, mod3, g, w, tabs, *, layer, tm, tn, tiles_per_mod, mod_row0, n0, n_tiles, att_width, name):
    m_rows, d = x2d.shape
    rope = tabs is not None
    n_q_tiles = max(att_width // tn - n0, 0)
    in_specs = [pl.BlockSpec((tm, d), lambda i, j: (i, 0)),
                pl.BlockSpec((1, 1, mod3.shape[-1]), lambda i, j: (mod_row0 + i // tiles_per_mod, 0, 0)),
                pl.BlockSpec((1, d), lambda i, j: (0, 0)),
                pl.BlockSpec((None, d, tn), lambda i, j: (layer, 0, n0 + j))]
    args = [x2d, mod3, g, w]
    if rope:
        tiles_per_seq = tabs[0].shape[1] // tm
        tspec = pl.BlockSpec((None, tm, LANES),
                             lambda i, j: (jnp.where(j < n_q_tiles, 0, 1), i % tiles_per_seq, 0))
        in_specs += [tspec, tspec, pl.BlockSpec((LANES, LANES), lambda i, j: (0, 0))]
        args += list(tabs)
    kern = functools.partial(_inproj_kernel, d_model=d, rope=rope,
                             n_rope_tiles=max(2 * att_width // tn - n0, 0), n_q_tiles=n_q_tiles,
                             row_chunk=min(256, tm))
    return pl.pallas_call(
        kern,
        grid=(m_rows // tm, n_tiles),
        in_specs=in_specs,
        out_specs=pl.BlockSpec((tm, tn), lambda i, j: (i, j)),
        out_shape=jax.ShapeDtypeStruct((m_rows, n_tiles * tn), BF16),
        scratch_shapes=[pltpu.VMEM((tm, d), BF16)],
        compiler_params=_cparams(2),
        name=name,
    )(*args)


def _attn_kernel(*refs, lc, ll, n_sub, heads, out_scale):
    if ll:
        q_ref, kc_ref, vc_ref, kl_ref, vl_ref, g_ref, lam_ref, sg_ref, out_ref, vt_ref, *slots = refs
        k_parts = [(kc_ref, 0, lc), (kl_ref, lc, ll)]
        v_parts = [(vc_ref, 0, lc), (vl_ref, lc, ll)]
    else:
        q_ref, kc_ref, vc_ref, g_ref, lam_ref, sg_ref, out_ref, vt_ref, *slots = refs
        k_parts = [(kc_ref, 0, lc)]
        v_parts = [(vc_ref, 0, lc)]
    lk = lc + ll
    st_ref = slots[0:2]
    pt_ref = slots[2:4]

    def head_lanes(hh):
        return slice(hh * LANES, (hh + 1) * LANES)

    @pl.when(pl.program_id(2) == 0)
    def _():
        for hh in range(heads):
            for ref, off, n in v_parts:
                for c in range(n // KV_CHUNK):
                    blk = ref[c * KV_CHUNK:(c + 1) * KV_CHUNK, head_lanes(hh)].astype(F32)
                    vt_ref[hh, :, off + c * KV_CHUNK:off + (c + 1) * KV_CHUNK] = blk.T.astype(BF16)

    row = lax.broadcasted_iota(jnp.int32, (ATT_VD, Q_SUB), 0)
    lam = lam_ref[0][0:1, :]
    sg = sg_ref[...]

    def item(t):
        if isinstance(t, int):
            return t // n_sub, pl.ds((t % n_sub) * Q_SUB, Q_SUB)
        assert heads == 1
        return 0, pl.ds(pl.multiple_of(t * Q_SUB, Q_SUB), Q_SUB)

    def scores(t, slot):
        hh, q_rows = item(t)
        qt = q_ref[q_rows, head_lanes(hh)].astype(F32).T
        qbd = jnp.concatenate([jnp.where(row < ATT_HD, qt, 0.0),
                               jnp.where(row >= ATT_HD, qt, 0.0)], axis=1).astype(BF16)
        m8 = None
        for k_ref, off, n in k_parts:
            for c in range(n // KV_CHUNK):
                kch = k_ref[c * KV_CHUNK:(c + 1) * KV_CHUNK, head_lanes(hh)]
                st = jnp.dot(kch, qbd, preferred_element_type=F32)
                st_ref[slot][off + c * KV_CHUNK:off + (c + 1) * KV_CHUNK, :] = st
                for r in range(KV_CHUNK // 8):
                    piece = st[r * 8:(r + 1) * 8, :]
                    m8 = piece if m8 is None else jnp.maximum(m8, piece)
        return jnp.max(m8, axis=0, keepdims=True)

    def softmax(slot, m):
        l8 = None
        marks = []
        m8 = jnp.broadcast_to(m, (8, 2 * Q_SUB))
        for g in range(lk // SM_GROUP):
            rows = slice(g * SM_GROUP, (g + 1) * SM_GROUP)
            m_use = m8 if g < SM_AHEAD else m8 + marks[g - SM_AHEAD] * 0.0
            p = jnp.exp2(st_ref[slot][rows, :] - jnp.tile(m_use, (SM_GROUP // 8, 1)))
            for r in range(SM_GROUP // 8):
                piece = p[r * 8:(r + 1) * 8, :]
                l8 = piece if l8 is None else l8 + piece
            marks.append(l8)
            pt_ref[slot][rows, :] = p.astype(BF16)
        return l8

    def pv(t, slot, l8):
        hh, q_rows = item(t)
        acc = jnp.dot(vt_ref[hh], pt_ref[slot][...], preferred_element_type=F32)
        o = acc * (1.0 / jnp.sum(l8, axis=0, keepdims=True))
        ot = o[:, :Q_SUB] - lam * o[:, Q_SUB:]
        oq = ot.T
        ms = jnp.mean(oq * oq, axis=-1, keepdims=True)
        y = oq * lax.rsqrt(ms + EPS) * sg * out_scale
        gate = g_ref[q_rows, head_lanes(hh)].astype(F32)
        out_ref[q_rows, head_lanes(hh)] = (y * _silu(gate)).astype(out_ref.dtype)

    def tick(t, slot, carry, do_scores=True, do_softmax=True, do_pv=True):
        m_prev, l8_prev = carry
        m = scores(t, slot) if do_scores else None
        l8 = softmax(1 - slot, m_prev) if do_softmax else None
        if do_pv:
            pv(t - 2, slot, l8_prev)
        return m, l8

    n_items = heads * n_sub
    n_trips = (n_items - 2) // TICKS_PER_TRIP if heads == 1 and n_items - 2 >= 2 * TICKS_PER_TRIP else 0
    n_loop = n_trips * TICKS_PER_TRIP
    carry = (None, None)
    t = 0
    while t < 2:
        carry = tick(t, t % 2, carry, do_scores=t < n_items, do_softmax=1 <= t <= n_items, do_pv=False)
        t += 1
    if n_loop:
        def trip(j, c):
            t0 = 2 + TICKS_PER_TRIP * j
            for u in range(TICKS_PER_TRIP):
                c = tick(t0 + u, u % 2, c)
            return c
        carry = lax.fori_loop(0, n_trips, trip, carry)
        t += n_loop
    while t < n_items + 2:
        carry = tick(t, t % 2, carry, do_scores=t < n_items, do_softmax=1 <= t <= n_items, do_pv=True)
        t += 1


def _attention(q_arr, kv_ctx, kv_lat, lam, subln_g, *, batch, lq, tq, heads, q_col, g_col, lc, kc_col, vc_col,
               ll, kl_col, vl_col, out_scale, name):
    nq = lq // tq
    width = heads * LANES
    assert ATT_HEADS % heads == 0 and all(c % heads == 0 for c in (q_col, g_col, kc_col, vc_col, kl_col, vl_col))
    col = lambda c: (lambda b, h, i: (b, c // heads + h))
    qcol = lambda c: (lambda b, h, i: (b * nq + i, c // heads + h))
    in_specs = [pl.BlockSpec((tq, width), qcol(q_col)),
                pl.BlockSpec((lc, width), col(kc_col)),
                pl.BlockSpec((lc, width), col(vc_col))]
    args = [q_arr, kv_ctx, kv_ctx]
    if kv_lat is not None:
        in_specs += [pl.BlockSpec((ll, width), col(kl_col)),
                     pl.BlockSpec((ll, width), col(vl_col))]
        args += [kv_lat, kv_lat]
    else:
        ll = 0
    in_specs += [pl.BlockSpec((tq, width), qcol(g_col)),
                 pl.BlockSpec((1, 8, LANES), lambda b, h, i: (0, 0, 0)),
                 pl.BlockSpec((1, ATT_VD), lambda b, h, i: (0, 0))]
    args += [q_arr, lam, subln_g]
    kern = functools.partial(_attn_kernel, lc=lc, ll=ll, n_sub=tq // Q_SUB, heads=heads, out_scale=out_scale)
    return pl.pallas_call(
        kern,
        grid=(batch, ATT_HEADS // heads, nq),
        in_specs=in_specs,
        out_specs=pl.BlockSpec((tq, width), lambda b, h, i: (b * nq + i, h)),
        out_shape=jax.ShapeDtypeStruct((batch * lq, ATT_HEADS * ATT_VD), BF16),
        scratch_shapes=[pltpu.VMEM((heads, ATT_VD, lc + ll), BF16),
                        pltpu.VMEM((lc + ll, 2 * Q_SUB), F32), pltpu.VMEM((lc + ll, 2 * Q_SUB), F32),
                        pltpu.VMEM((lc + ll, 2 * Q_SUB), BF16), pltpu.VMEM((lc + ll, 2 * Q_SUB), BF16)],
        compiler_params=_cparams(3),
        name=name,
    )(*args)


def _poolconv_kernel(up_c, up_p, up_n, gp_ref, a_c, a_p, a_n, b_c, b_p, b_n, gc_ref,
                     wpool_ref, pscale_ref, wdw_ref, bdw_ref, lng_ref, lnb_ref, wpw_ref,
                     ypool_ref, yconv_ref, u_ref, p_ref, z_ref, s_ref, r_ref, *, rows, tiles_per_seq, seq_len,
                     row_blk):
    iseq = pl.program_id(0) % tiles_per_seq
    has_prev = iseq > 0
    has_next = iseq < tiles_per_seq - 1

    def glu(a_ref, b_ref):
        return a_ref[...].astype(F32) * jax.nn.sigmoid(b_ref[...].astype(F32))

    u_ref[0:HALO, :] = jnp.where(has_prev, glu(a_p, b_p), 0.0)
    u_ref[HALO:HALO + rows, :] = glu(a_c, b_c)
    u_ref[HALO + rows:, :] = jnp.where(has_next, glu(a_n, b_n), 0.0)
    p_ref[0:HALO, :] = jnp.where(has_prev, up_p[...].astype(F32), 0.0)
    p_ref[HALO:HALO + rows, :] = up_c[...].astype(F32)
    p_ref[HALO + rows:, :] = jnp.where(has_next, up_n[...].astype(F32), 0.0)

    ext = rows + 2 * HALO - 8
    for b in range(1, 8):
        s_ref[b - 1] = u_ref[b:b + ext, :]

    for rb in range(rows // row_blk):
        acc = None
        for j in range(CONV_K):
            off = HALO - CONV_K // 2 + j
            base = rb * row_blk + (off // 8) * 8
            if off % 8 == 0:
                tap = u_ref[base:base + row_blk, :]
            else:
                tap = s_ref[off % 8 - 1, base:base + row_blk, :]
            term = tap * wdw_ref[j:j + 1, :]
            acc = term if acc is None else acc + term
        y = acc + bdw_ref[...]
        mu = jnp.mean(y, axis=-1, keepdims=True)
        yc = y - mu
        var = jnp.mean(yc * yc, axis=-1, keepdims=True)
        z = yc * lax.rsqrt(var + EPS) * lng_ref[...] + lnb_ref[...]
        z_ref[rb * row_blk:(rb + 1) * row_blk, :] = _silu(z).astype(BF16)
    yconv = jnp.dot(z_ref[...], wpw_ref[...], preferred_element_type=F32)
    yconv_ref[...] = (yconv * _silu(gc_ref[...].astype(F32))).astype(yconv_ref.dtype)

    t = iseq * rows + lax.broadcasted_iota(jnp.int32, (rows, POOL_GD), 0)
    for g, w in enumerate(POOL_WINDOWS):
        cols = slice(g * POOL_GD, (g + 1) * POOL_GD)
        hw = w // 2
        load = lambda s, m, _c=cols: p_ref[s:s + m, _c]
        n, level = 1, 0
        while n < hw:
            dst = r_ref.at[level % 2]
            dst[8:rows + HALO + 8, :] = load(8, rows + HALO) + load(8 + n, rows + HALO)
            dst[rows + HALO + 8:, :] = jnp.zeros((8, POOL_GD), F32)
            load = lambda s, m, _d=dst: _d[s:s + m, :]
            n, level = 2 * n, level + 1
        ssum = load(HALO - hw, rows) + load(HALO, rows)
        cnt = (jnp.minimum(t + hw, seq_len) - jnp.maximum(t - hw, 0)).astype(F32)
        dlt = ssum / cnt - p_ref[HALO:HALO + rows, cols]
        yp = jnp.dot(dlt.astype(BF16), wpool_ref[g], preferred_element_type=F32) * pscale_ref[:, cols]
        ypool_ref[:, cols] = (yp * _silu(gp_ref[:, cols].astype(F32))).astype(ypool_ref.dtype)


def _poolconv(proj, wpool, pscale, wdw, bdw, lng, lnb, wpw, *, seq_len, rows, col0, width, name):
    m_rows = proj.shape[0]
    tiles_per_seq = seq_len // rows
    hpr = rows // HALO
    n_halo = m_rows // HALO
    cb = col0 // width

    def cur(k):
        return pl.BlockSpec((rows, width), lambda i: (i, cb + k))

    def prev(k):
        return pl.BlockSpec((HALO, width), lambda i: (jnp.maximum(i * hpr - 1, 0), cb + k))

    def nxt(k):
        return pl.BlockSpec((HALO, width), lambda i: (jnp.minimum((i + 1) * hpr, n_halo - 1), cb + k))

    full = lambda a: pl.BlockSpec(a.shape, lambda i: (0,) * a.ndim)
    consts = [wpool, pscale, wdw, bdw, lng, lnb, wpw]
    kern = functools.partial(_poolconv_kernel, rows=rows, tiles_per_seq=tiles_per_seq, seq_len=seq_len,
                             row_blk=min(64, rows))
    return pl.pallas_call(
        kern,
        grid=(m_rows // rows,),
        in_specs=[cur(0), prev(0), nxt(0), cur(1), cur(2), prev(2), nxt(2), cur(3), prev(3), nxt(3), cur(4)]
                 + [full(a) for a in consts],
        out_specs=[pl.BlockSpec((rows, width), lambda i: (i, 0)),
                   pl.BlockSpec((rows, width), lambda i: (i, 0))],
        out_shape=[jax.ShapeDtypeStruct((m_rows, width), BF16),
                   jax.ShapeDtypeStruct((m_rows, width), BF16)],
        scratch_shapes=[pltpu.VMEM((rows + 2 * HALO, width), F32),
                        pltpu.VMEM((rows + 2 * HALO, width), F32),
                        pltpu.VMEM((rows, width), BF16),
                        pltpu.VMEM((7, rows + 2 * HALO - 8, width), F32),
                        pltpu.VMEM((2, rows + 2 * HALO, POOL_GD), F32)],
        compiler_params=_cparams(1),
        name=name,
    )(*([proj] * 11), *consts)


def _outproj_kernel(*refs, d_model, att_width, pool_width, final):
    if final:
        ya_ref, yp_ref, yc_ref, w_ref, x_ref, mod_ref, fg_ref, out_ref = refs
    else:
        ya_ref, yp_ref, yc_ref, w_ref, x_ref, mod_ref, out_ref = refs
    c1 = att_width
    c2 = att_width + pool_width
    y = jnp.dot(ya_ref[...], w_ref[0:c1, :].astype(BF16), preferred_element_type=F32)
    y = y + jnp.dot(yp_ref[...], w_ref[c1:c2, :].astype(BF16), preferred_element_type=F32)
    y = y + jnp.dot(yc_ref[...], w_ref[c2:, :].astype(BF16), preferred_element_type=F32)
    gate = mod_ref[0][:, 2 * d_model:]
    xn = x_ref[...] + gate * y
    if final:
        ms = jnp.mean(xn * xn, axis=-1, keepdims=True)
        xn = xn * lax.rsqrt(ms + EPS) * fg_ref[...]
    out_ref[...] = xn


def _outproj(ya, yp, yc, w, x2d, mod3, final_g, *, layer, tm, tiles_per_mod, mod_row0, name):
    m_rows, d = x2d.shape
    final = final_g is not None
    in_specs = [pl.BlockSpec((tm, ya.shape[1]), lambda i: (i, 0)),
                pl.BlockSpec((tm, yp.shape[1]), lambda i: (i, 0)),
                pl.BlockSpec((tm, yc.shape[1]), lambda i: (i, 0)),
                pl.BlockSpec((None,) + w.shape[1:], lambda i: (layer, 0, 0), pipeline_mode=pl.Buffered(1)),
                pl.BlockSpec((tm, d), lambda i: (i, 0)),
                pl.BlockSpec((1, 1, mod3.shape[-1]), lambda i: (mod_row0 + i // tiles_per_mod, 0, 0))]
    args = [ya, yp, yc, w, x2d, mod3]
    if final:
        in_specs.append(pl.BlockSpec((1, d), lambda i: (0, 0)))
        args.append(final_g)
    kern = functools.partial(_outproj_kernel, d_model=d, att_width=ya.shape[1], pool_width=yp.shape[1],
                             final=final)
    return pl.pallas_call(
        kern,
        grid=(m_rows // tm,),
        in_specs=in_specs,
        out_specs=pl.BlockSpec((tm, d), lambda i: (i, 0)),
        out_shape=jax.ShapeDtypeStruct((m_rows, d), F32),
        compiler_params=_cparams(1),
        name=name,
    )(*args)


def _rope_tables(seq_len):
    n_freq = ATT_HD // 4
    inv_freq = (np.float32(ROPE_BASE) ** (-np.arange(n_freq, dtype=np.float32) / n_freq)).astype(np.float32)
    t = np.arange(seq_len)
    lane = np.arange(LANES)
    in_col_half = (lane % ATT_HD) >= ATT_HD // 2
    pos = np.where(in_col_half[None, :], (t % GRID_W)[:, None], (t // GRID_W)[:, None]).astype(np.float32)
    ang = pos * inv_freq[lane % n_freq][None, :]
    cos = np.cos(ang).astype(np.float32)
    sin = np.sin(ang).astype(np.float32)
    qs = np.float32(ATT_HD ** -0.5 * math.log2(math.e))
    perm = np.zeros((LANES, LANES), np.float32)
    for p in range(LANES):
        if p % (2 * n_freq) < n_freq:
            perm[p + n_freq, p] = -1.0
        else:
            perm[p - n_freq, p] = 1.0
    return (jnp.asarray(np.stack([cos * qs, cos])), jnp.asarray(np.stack([sin * qs, sin])),
            jnp.asarray(perm, dtype=BF16))


def kernel(x, c, ctx, c_ctx, w_mod, b_mod, norm_g, w_in, lambda_q1, lambda_k1, lambda_q2, lambda_k2,
           subln_g, w_pool, pool_scale, w_dw, b_dw, conv_ln_g, conv_ln_b, w_pw2, w_out, final_g):
    batch, seq, d = x.shape
    lc = ctx.shape[1]
    depth = w_mod.shape[0]
    att_w = d // 2
    pool_w = d // 4
    assert batch < 8 and seq % GRID_W == 0 and att_w == ATT_HEADS * ATT_VD and pool_w == 4 * POOL_GD
    assert seq % KV_CHUNK == 0 and lc % KV_CHUNK == 0

    cc = jnp.concatenate([c, c_ctx[None, :], jnp.zeros((8 - batch - 1, d), F32)], axis=0)
    lam_init = [0.8 - 0.6 * math.exp(-0.3 * l) for l in range(depth)]
    li = jnp.broadcast_to(jnp.asarray(lam_init, F32)[:, None, None], (depth, 1, LANES))
    mod, lam = _modulation(cc, w_mod, b_mod, lambda_q1, lambda_k1, lambda_q2, lambda_k2, li)
    tabs = _rope_tables(seq)

    tm_lat = min(1024, seq)
    tn = 512
    n_in = w_in.shape[-1]
    hcol = att_w // LANES
    pc_col0 = 4 * att_w
    x2d = x.reshape(batch * seq, d)
    c2d = ctx.reshape(batch * lc, d)
    row = lambda a: a.reshape(1, -1)

    for l in range(depth):
        last = l == depth - 1
        mod3 = mod[l].reshape(8, 1, 3 * d)
        lw = (w_pool[l].astype(BF16), row(pool_scale[l]), w_dw[l], row(b_dw[l]), row(conv_ln_g[l]),
              row(conv_ln_b[l]), w_pw2[l].astype(BF16))
        g_l = row(norm_g[l])
        out_scale = 1.0 - lam_init[l]

        proj = _inproj(x2d, mod3, g_l, w_in, tabs, layer=l, tm=tm_lat, tn=tn, tiles_per_mod=seq // tm_lat,
                       mod_row0=0, n0=0, n_tiles=n_in // tn, att_width=att_w, name=f"inproj_lat{l}")
        if last:
            cproj = _inproj(c2d, mod3, g_l, w_in, None, layer=l, tm=batch * lc, tn=tn, tiles_per_mod=1,
                            mod_row0=batch, n0=att_w // tn, n_tiles=2 * att_w // tn, att_width=att_w,
                            name=f"inproj_ctx{l}")
            kc_col, vc_col = 0, hcol
        else:
            cproj = _inproj(c2d, mod3, g_l, w_in, None, layer=l, tm=batch * lc, tn=tn, tiles_per_mod=1,
                            mod_row0=batch, n0=0, n_tiles=n_in // tn, att_width=att_w,
                            name=f"inproj_ctx{l}")
            kc_col, vc_col = hcol, 2 * hcol

        y_att = _attention(proj, cproj, proj, lam[l:l + 1], row(subln_g[l]), batch=batch, lq=seq,
                           tq=seq, heads=1, q_col=0, g_col=3 * hcol, lc=lc, kc_col=kc_col, vc_col=vc_col,
                           ll=seq, kl_col=hcol, vl_col=2 * hcol, out_scale=out_scale, name=f"attn_lat{l}")
        y_pool, y_conv = _poolconv(proj, *lw, seq_len=seq, rows=min(512, seq), col0=pc_col0, width=pool_w,
                                   name=f"poolconv_lat{l}")
        if not last:
            yc_att = _attention(cproj, cproj, None, lam[l:l + 1], row(subln_g[l]), batch=batch, lq=lc,
                                tq=lc, heads=ATT_HEADS, q_col=0, g_col=3 * hcol, lc=lc, kc_col=kc_col, vc_col=vc_col,
                                ll=0, kl_col=0, vl_col=0, out_scale=out_scale, name=f"attn_ctx{l}")
            yc_pool, yc_conv = _poolconv(cproj, *lw, seq_len=lc, rows=min(256, lc), col0=pc_col0,
                                         width=pool_w, name=f"poolconv_ctx{l}")
            c2d = _outproj(yc_att, yc_pool, yc_conv, w_out, c2d, mod3, None, layer=l, tm=min(512, batch * lc),
                           tiles_per_mod=batch * lc, mod_row0=batch, name=f"outproj_ctx{l}")
        tm_o = min(512, seq)
        x2d = _outproj(y_att, y_pool, y_conv, w_out, x2d, mod3, row(final_g) if last else None,
                       layer=l, tm=tm_o, tiles_per_mod=seq // tm_o, mod_row0=0, name=f"outproj_lat{l}")
    return x2d.reshape(batch, seq, d)
```

```python
import functools
import math

import jax
import jax.numpy as jnp
import numpy as np
from jax import lax
from jax.experimental import pallas as pl
from jax.experimental.pallas import tpu as pltpu

ATT_HEADS = 8
ATT_HD = 64
ATT_VD = 2 * ATT_HD
POOL_WINDOWS = (2, 4, 8, 16)
POOL_GD = 128
CONV_K = 31
GRID_W = 64
ROPE_BASE = 10000.0
EPS = 1e-6

LANES = 128
KV_CHUNK = 256
Q_SUB = 128
LAT_HEADS_PER_STEP = 2
SM_GROUP = 32
SM_AHEAD = 2
TICKS_PER_TRIP = 14
HALO = 16
VMEM_LIMIT = 56 * 1024 * 1024

F32 = jnp.float32
BF16 = jnp.bfloat16


def _cparams(n_axes, flags=None):
    return pltpu.CompilerParams(dimension_semantics=("arbitrary",) * n_axes,
                                vmem_limit_bytes=VMEM_LIMIT, flags=flags)


def _silu(v):
    return v * jax.nn.sigmoid(v)


def _mod_kernel(cc_ref, w_ref, b_ref, lq1_ref, lk1_ref, lq2_ref, lk2_ref, li_ref, out_ref, lam_ref):
    s = _silu(cc_ref[...])
    out_ref[0] = jnp.dot(s.astype(BF16), w_ref[0].astype(BF16), preferred_element_type=F32) + b_ref[0]
    a1 = jnp.sum(lq1_ref[0] * lk1_ref[0], axis=-1, keepdims=True)
    a2 = jnp.sum(lq2_ref[0] * lk2_ref[0], axis=-1, keepdims=True)
    lam = jnp.exp(a1) - jnp.exp(a2) + li_ref[0]
    lam_ref[0] = jnp.broadcast_to(lam, lam_ref.shape[1:])


def _modulation(cc, w_mod, b_mod, lq1, lk1, lq2, lk2, lam_init):
    depth, d, n = w_mod.shape
    tn = n // 4 if n % (4 * LANES) == 0 else n
    vec = lambda a: a.reshape(depth, 1, a.shape[-1])
    lspec = pl.BlockSpec((1, 1, ATT_HD), lambda l, j: (l, 0, 0))
    return pl.pallas_call(
        _mod_kernel,
        grid=(depth, n // tn),
        in_specs=[pl.BlockSpec((8, d), lambda l, j: (0, 0)),
                  pl.BlockSpec((1, d, tn), lambda l, j: (l, 0, j)),
                  pl.BlockSpec((1, 1, tn), lambda l, j: (l, 0, j)),
                  lspec, lspec, lspec, lspec,
                  pl.BlockSpec((1, 1, LANES), lambda l, j: (l, 0, 0))],
        out_specs=[pl.BlockSpec((1, 8, tn), lambda l, j: (l, 0, j)),
                   pl.BlockSpec((1, 8, LANES), lambda l, j: (l, 0, 0))],
        out_shape=[jax.ShapeDtypeStruct((depth, 8, n), F32),
                   jax.ShapeDtypeStruct((depth, 8, LANES), F32)],
        compiler_params=_cparams(2),
        name="modulation",
    )(cc, w_mod, vec(b_mod), vec(lq1), vec(lk1), vec(lq2), vec(lk2), lam_init)


def _inproj_kernel(*refs, d_model, rope, n_rope_tiles, n_q_tiles, row_chunk):
    if rope:
        x_ref, mod_ref, g_ref, w_ref, cos_ref, sin_ref, perm_ref, out_ref, h_ref = refs
    else:
        x_ref, mod_ref, g_ref, w_ref, out_ref, h_ref = refs
    j = pl.program_id(1)
    tm, tn = out_ref.shape

    def rope_store(rows, acc):
        cos = cos_ref[rows, :]
        sin = sin_ref[rows, :]
        for hh in range(tn // LANES):
            t = acc[:, hh * LANES:(hh + 1) * LANES]
            rot = jnp.dot(t.astype(BF16), perm_ref[...], preferred_element_type=F32)
            out_ref[rows, hh * LANES:(hh + 1) * LANES] = (t * cos + rot * sin).astype(out_ref.dtype)

    def plain_store(rows, acc):
        qs = jnp.where(j < n_q_tiles, ATT_HD ** -0.5 * math.log2(math.e), 1.0).astype(F32)
        out_ref[rows, :] = (acc * qs).astype(out_ref.dtype)

    @pl.when(j == 0)
    def _():
        m = mod_ref[0]
        shift = m[:, :d_model]
        gain = g_ref[...] * (1.0 + m[:, d_model:2 * d_model])
        wb = w_ref[...].astype(BF16)
        for c in range(tm // row_chunk):
            rows = slice(c * row_chunk, (c + 1) * row_chunk)
            xf = x_ref[rows, :]
            ms = jnp.mean(xf * xf, axis=-1, keepdims=True)
            h = (xf * lax.rsqrt(ms + EPS) * gain + shift).astype(BF16)
            h_ref[rows, :] = h
            acc = jnp.dot(h, wb, preferred_element_type=F32)
            if rope:
                rope_store(rows, acc)
            else:
                plain_store(rows, acc)

    if rope and n_rope_tiles > 1:
        @pl.when(jnp.logical_and(j > 0, j < n_rope_tiles))
        def _():
            rope_store(slice(None), jnp.dot(h_ref[...], w_ref[...].astype(BF16), preferred_element_type=F32))

    @pl.when(j >= (n_rope_tiles if rope else 1))
    def _():
        plain_store(slice(None), jnp.dot(h_ref[...], w_ref[...].astype(BF16), preferred_element_type=F32))


def _inproj(x2d, mod3, g, w, tabs, *, layer, tm, tn, tiles_per_mod, mod_row0, n0, n_tiles, att_width, name):
    m_rows, d = x2d.shape
    rope = tabs is not None
    n_q_tiles = max(att_width // tn - n0, 0)
    in_specs = [pl.BlockSpec((tm, d), lambda i, j: (i, 0)),
                pl.BlockSpec((1, 1, mod3.shape[-1]), lambda i, j: (mod_row0 + i // tiles_per_mod, 0, 0)),
                pl.BlockSpec((1, d), lambda i, j: (0, 0)),
                pl.BlockSpec((None, d, tn), lambda i, j: (layer, 0, n0 + j))]
    args = [x2d, mod3, g, w]
    if rope:
        tiles_per_seq = tabs[0].shape[1] // tm
        tspec = pl.BlockSpec((None, tm, LANES),
                             lambda i, j: (jnp.where(j < n_q_tiles, 0, 1), i % tiles_per_seq, 0))
        in_specs += [tspec, tspec, pl.BlockSpec((LANES, LANES), lambda i, j: (0, 0))]
        args += list(tabs)
    kern = functools.partial(_inproj_kernel, d_model=d, rope=rope,
                             n_rope_tiles=max(2 * att_width // tn - n0, 0), n_q_tiles=n_q_tiles,
                             row_chunk=min(256, tm))
    return pl.pallas_call(
        kern,
        grid=(m_rows // tm, n_tiles),
        in_specs=in_specs,
        out_specs=pl.BlockSpec((tm, tn), lambda i, j: (i, j)),
        out_shape=jax.ShapeDtypeStruct((m_rows, n_tiles * tn), BF16),
        scratch_shapes=[pltpu.VMEM((tm, d), BF16)],
        compiler_params=_cparams(2),
        name=name,
    )(*args)


def _attn_kernel(*refs, lc, ll, n_sub, heads, out_scale):
    if ll:
        q_ref, kc_ref, vc_ref, kl_ref, vl_ref, g_ref, lam_ref, sg_ref, out_ref, vt_ref, *slots = refs
        k_parts = [(kc_ref, 0, lc), (kl_ref, lc, ll)]
        v_parts = [(vc_ref, 0, lc), (vl_ref, lc, ll)]
    else:
        q_ref, kc_ref, vc_ref, g_ref, lam_ref, sg_ref, out_ref, vt_ref, *slots = refs
        k_parts = [(kc_ref, 0, lc)]
        v_parts = [(vc_ref, 0, lc)]
    lk = lc + ll
    st_ref = slots[0:2]
    pt_ref = slots[2:4]

    def head_lanes(hh):
        return slice(hh * LANES, (hh + 1) * LANES)

    for hh in range(heads):
        for ref, off, n in v_parts:
            for c in range(n // KV_CHUNK):
                blk = ref[c * KV_CHUNK:(c + 1) * KV_CHUNK, head_lanes(hh)].astype(F32)
                vt_ref[hh, :, off + c * KV_CHUNK:off + (c + 1) * KV_CHUNK] = blk.T.astype(BF16)

    row = lax.broadcasted_iota(jnp.int32, (ATT_VD, Q_SUB), 0)
    lam = lam_ref[0][0:1, :]
    sg = sg_ref[...]

    def item(t):
        if isinstance(t, int):
            return t // n_sub, pl.ds((t % n_sub) * Q_SUB, Q_SUB)
        assert heads == 1
        return 0, pl.ds(pl.multiple_of(t * Q_SUB, Q_SUB), Q_SUB)

    def scores(t, slot):
        hh, q_rows = item(t)
        qt = q_ref[q_rows, head_lanes(hh)].astype(F32).T
        qbd = jnp.concatenate([jnp.where(row < ATT_HD, qt, 0.0),
                               jnp.where(row >= ATT_HD, qt, 0.0)], axis=1).astype(BF16)
        m8 = None
        for k_ref, off, n in k_parts:
            for c in range(n // KV_CHUNK):
                kch = k_ref[c * KV_CHUNK:(c + 1) * KV_CHUNK, head_lanes(hh)]
                st = jnp.dot(kch, qbd, preferred_element_type=F32)
                st_ref[slot][off + c * KV_CHUNK:off + (c + 1) * KV_CHUNK, :] = st
                for r in range(KV_CHUNK // 8):
                    piece = st[r * 8:(r + 1) * 8, :]
                    m8 = piece if m8 is None else jnp.maximum(m8, piece)
        return jnp.max(m8, axis=0, keepdims=True)

    def softmax(slot, m):
        l8 = None
        marks = []
        m8 = jnp.broadcast_to(m, (8, 2 * Q_SUB))
        for g in range(lk // SM_GROUP):
            rows = slice(g * SM_GROUP, (g + 1) * SM_GROUP)
            m_use = m8 if g < SM_AHEAD else m8 + marks[g - SM_AHEAD] * 0.0
            p = jnp.exp2(st_ref[slot][rows, :] - jnp.tile(m_use, (SM_GROUP // 8, 1)))
            for r in range(SM_GROUP // 8):
                piece = p[r * 8:(r + 1) * 8, :]
                l8 = piece if l8 is None else l8 + piece
            marks.append(l8)
            pt_ref[slot][rows, :] = p.astype(BF16)
        return l8

    def pv(t, slot, l8):
        hh, q_rows = item(t)
        acc = jnp.dot(vt_ref[hh], pt_ref[slot][...], preferred_element_type=F32)
        o = acc * (1.0 / jnp.sum(l8, axis=0, keepdims=True))
        ot = o[:, :Q_SUB] - lam * o[:, Q_SUB:]
        oq = ot.T
        ms = jnp.mean(oq * oq, axis=-1, keepdims=True)
        y = oq * lax.rsqrt(ms + EPS) * sg * out_scale
        gate = g_ref[q_rows, head_lanes(hh)].astype(F32)
        out_ref[q_rows, head_lanes(hh)] = (y * _silu(gate)).astype(out_ref.dtype)

    def tick(t, slot, carry, do_scores=True, do_softmax=True, do_pv=True):
        m_prev, l8_prev = carry
        m = scores(t, slot) if do_scores else None
        l8 = softmax(1 - slot, m_prev) if do_softmax else None
        if do_pv:
            pv(t - 2, slot, l8_prev)
        return m, l8

    n_items = heads * n_sub
    n_trips = (n_items - 2) // TICKS_PER_TRIP if heads == 1 and n_items - 2 >= 2 * TICKS_PER_TRIP else 0
    n_loop = n_trips * TICKS_PER_TRIP
    carry = (None, None)
    t = 0
    while t < 2:
        carry = tick(t, t % 2, carry, do_scores=t < n_items, do_softmax=1 <= t <= n_items, do_pv=False)
        t += 1
    if n_loop:
        def trip(j, c):
            t0 = 2 + TICKS_PER_TRIP * j
            for u in range(TICKS_PER_TRIP):
                c = tick(t0 + u, u % 2, c)
            return c
        carry = lax.fori_loop(0, n_trips, trip, carry)
        t += n_loop
    while t < n_items + 2:
        carry = tick(t, t % 2, carry, do_scores=t < n_items, do_softmax=1 <= t <= n_items, do_pv=True)
        t += 1


def _attention(q_arr, kv_ctx, kv_lat, lam, subln_g, *, batch, lq, heads, q_col, g_col, lc, kc_col, vc_col,
               ll, kl_col, vl_col, out_scale, name):
    width = heads * LANES
    assert ATT_HEADS % heads == 0 and all(c % heads == 0 for c in (q_col, g_col, kc_col, vc_col, kl_col, vl_col))
    col = lambda c: (lambda b, h: (b, c // heads + h))
    in_specs = [pl.BlockSpec((lq, width), col(q_col)),
                pl.BlockSpec((lc, width), col(kc_col)),
                pl.BlockSpec((lc, width), col(vc_col))]
    args = [q_arr, kv_ctx, kv_ctx]
    if kv_lat is not None:
        in_specs += [pl.BlockSpec((ll, width), col(kl_col)),
                     pl.BlockSpec((ll, width), col(vl_col))]
        args += [kv_lat, kv_lat]
    else:
        ll = 0
    in_specs += [pl.BlockSpec((lq, width), col(g_col)),
                 pl.BlockSpec((1, 8, LANES), lambda b, h: (0, 0, 0)),
                 pl.BlockSpec((1, ATT_VD), lambda b, h: (0, 0))]
    args += [q_arr, lam, subln_g]
    kern = functools.partial(_attn_kernel, lc=lc, ll=ll, n_sub=lq // Q_SUB, heads=heads, out_scale=out_scale)
    return pl.pallas_call(
        kern,
        grid=(batch, ATT_HEADS // heads),
        in_specs=in_specs,
        out_specs=pl.BlockSpec((lq, width), lambda b, h: (b, h)),
        out_shape=jax.ShapeDtypeStruct((batch * lq, ATT_HEADS * ATT_VD), BF16),
        scratch_shapes=[pltpu.VMEM((heads, ATT_VD, lc + ll), BF16),
                        pltpu.VMEM((lc + ll, 2 * Q_SUB), F32), pltpu.VMEM((lc + ll, 2 * Q_SUB), F32),
                        pltpu.VMEM((lc + ll, 2 * Q_SUB), BF16), pltpu.VMEM((lc + ll, 2 * Q_SUB), BF16)],
        compiler_params=_cparams(2),
        name=name,
    )(*args)


def _poolconv_kernel(up_c, up_p, up_n, gp_ref, a_c, a_p, a_n, b_c, b_p, b_n, gc_ref,
                     wpool_ref, pscale_ref, wdw_ref, bdw_ref, lng_ref, lnb_ref, wpw_ref,
                     ypool_ref, yconv_ref, u_ref, p_ref, z_ref, s_ref, r_ref, *, rows, tiles_per_seq, seq_len,
                     row_blk):
    iseq = pl.program_id(0) % tiles_per_seq
    has_prev = iseq > 0
    has_next = iseq < tiles_per_seq - 1

    def glu(a_ref, b_ref):
        return a_ref[...].astype(F32) * jax.nn.sigmoid(b_ref[...].astype(F32))

    u_ref[0:HALO, :] = jnp.where(has_prev, glu(a_p, b_p), 0.0)
    u_ref[HALO:HALO + rows, :] = glu(a_c, b_c)
    u_ref[HALO + rows:, :] = jnp.where(has_next, glu(a_n, b_n), 0.0)
    p_ref[0:HALO, :] = jnp.where(has_prev, up_p[...].astype(F32), 0.0)
    p_ref[HALO:HALO + rows, :] = up_c[...].astype(F32)
    p_ref[HALO + rows:, :] = jnp.where(has_next, up_n[...].astype(F32), 0.0)

    ext = rows + 2 * HALO - 8
    for b in range(1, 8):
        s_ref[b - 1] = u_ref[b:b + ext, :]

    for rb in range(rows // row_blk):
        acc = None
        for j in range(CONV_K):
            off = HALO - CONV_K // 2 + j
            base = rb * row_blk + (off // 8) * 8
            if off % 8 == 0:
                tap = u_ref[base:base + row_blk, :]
            else:
                tap = s_ref[off % 8 - 1, base:base + row_blk, :]
            term = tap * wdw_ref[j:j + 1, :]
            acc = term if acc is None else acc + term
        y = acc + bdw_ref[...]
        mu = jnp.mean(y, axis=-1, keepdims=True)
        yc = y - mu
        var = jnp.mean(yc * yc, axis=-1, keepdims=True)
        z = yc * lax.rsqrt(var + EPS) * lng_ref[...] + lnb_ref[...]
        z_ref[rb * row_blk:(rb + 1) * row_blk, :] = _silu(z).astype(BF16)
    yconv = jnp.dot(z_ref[...], wpw_ref[...], preferred_element_type=F32)
    yconv_ref[...] = (yconv * _silu(gc_ref[...].astype(F32))).astype(yconv_ref.dtype)

    t = iseq * rows + lax.broadcasted_iota(jnp.int32, (rows, POOL_GD), 0)
    for g, w in enumerate(POOL_WINDOWS):
        cols = slice(g * POOL_GD, (g + 1) * POOL_GD)
        hw = w // 2
        load = lambda s, m, _c=cols: p_ref[s:s + m, _c]
        n, level = 1, 0
        while n < hw:
            dst = r_ref.at[level % 2]
            dst[8:rows + HALO + 8, :] = load(8, rows + HALO) + load(8 + n, rows + HALO)
            dst[rows + HALO + 8:, :] = jnp.zeros((8, POOL_GD), F32)
            load = lambda s, m, _d=dst: _d[s:s + m, :]
            n, level = 2 * n, level + 1
        ssum = load(HALO - hw, rows) + load(HALO, rows)
        cnt = (jnp.minimum(t + hw, seq_len) - jnp.maximum(t - hw, 0)).astype(F32)
        dlt = ssum / cnt - p_ref[HALO:HALO + rows, cols]
        yp = jnp.dot(dlt.astype(BF16), wpool_ref[g], preferred_element_type=F32) * pscale_ref[:, cols]
        ypool_ref[:, cols] = (yp * _silu(gp_ref[:, cols].astype(F32))).astype(ypool_ref.dtype)


def _poolconv(proj, wpool, pscale, wdw, bdw, lng, lnb, wpw, *, seq_len, rows, col0, width, name):
    m_rows = proj.shape[0]
    tiles_per_seq = seq_len // rows
    hpr = rows // HALO
    n_halo = m_rows // HALO
    cb = col0 // width

    def cur(k):
        return pl.BlockSpec((rows, width), lambda i: (i, cb + k))

    def prev(k):
        return pl.BlockSpec((HALO, width), lambda i: (jnp.maximum(i * hpr - 1, 0), cb + k))

    def nxt(k):
        return pl.BlockSpec((HALO, width), lambda i: (jnp.minimum((i + 1) * hpr, n_halo - 1), cb + k))

    full = lambda a: pl.BlockSpec(a.shape, lambda i: (0,) * a.ndim)
    consts = [wpool, pscale, wdw, bdw, lng, lnb, wpw]
    kern = functools.partial(_poolconv_kernel, rows=rows, tiles_per_seq=tiles_per_seq, seq_len=seq_len,
                             row_blk=min(64, rows))
    return pl.pallas_call(
        kern,
        grid=(m_rows // rows,),
        in_specs=[cur(0), prev(0), nxt(0), cur(1), cur(2), prev(2), nxt(2), cur(3), prev(3), nxt(3), cur(4)]
                 + [full(a) for a in consts],
        out_specs=[pl.BlockSpec((rows, width), lambda i: (i, 0)),
                   pl.BlockSpec((rows, width), lambda i: (i, 0))],
        out_shape=[jax.ShapeDtypeStruct((m_rows, width), BF16),
                   jax.ShapeDtypeStruct((m_rows, width), BF16)],
        scratch_shapes=[pltpu.VMEM((rows + 2 * HALO, width), F32),
                        pltpu.VMEM((rows + 2 * HALO, width), F32),
                        pltpu.VMEM((rows, width), BF16),
                        pltpu.VMEM((7, rows + 2 * HALO - 8, width), F32),
                        pltpu.VMEM((2, rows + 2 * HALO, POOL_GD), F32)],
        compiler_params=_cparams(1),
        name=name,
    )(*([proj] * 11), *consts)


def _outproj_kernel(*refs, d_model, att_width, pool_width, final):
    if final:
        ya_ref, yp_ref, yc_ref, w_ref, x_ref, mod_ref, fg_ref, out_ref = refs
    else:
        ya_ref, yp_ref, yc_ref, w_ref, x_ref, mod_ref, out_ref = refs
    c1 = att_width
    c2 = att_width + pool_width
    y = jnp.dot(ya_ref[...], w_ref[0:c1, :].astype(BF16), preferred_element_type=F32)
    y = y + jnp.dot(yp_ref[...], w_ref[c1:c2, :].astype(BF16), preferred_element_type=F32)
    y = y + jnp.dot(yc_ref[...], w_ref[c2:, :].astype(BF16), preferred_element_type=F32)
    gate = mod_ref[0][:, 2 * d_model:]
    xn = x_ref[...] + gate * y
    if final:
        ms = jnp.mean(xn * xn, axis=-1, keepdims=True)
        xn = xn * lax.rsqrt(ms + EPS) * fg_ref[...]
    out_ref[...] = xn


def _outproj(ya, yp, yc, w, x2d, mod3, final_g, *, layer, tm, tiles_per_mod, mod_row0, name):
    m_rows, d = x2d.shape
    final = final_g is not None
    in_specs = [pl.BlockSpec((tm, ya.shape[1]), lambda i: (i, 0)),
                pl.BlockSpec((tm, yp.shape[1]), lambda i: (i, 0)),
                pl.BlockSpec((tm, yc.shape[1]), lambda i: (i, 0)),
                pl.BlockSpec((None,) + w.shape[1:], lambda i: (layer, 0, 0), pipeline_mode=pl.Buffered(1)),
                pl.BlockSpec((tm, d), lambda i: (i, 0)),
                pl.BlockSpec((1, 1, mod3.shape[-1]), lambda i: (mod_row0 + i // tiles_per_mod, 0, 0))]
    args = [ya, yp, yc, w, x2d, mod3]
    if final:
        in_specs.append(pl.BlockSpec((1, d), lambda i: (0, 0)))
        args.append(final_g)
    kern = functools.partial(_outproj_kernel, d_model=d, att_width=ya.shape[1], pool_width=yp.shape[1],
                             final=final)
    return pl.pallas_call(
        kern,
        grid=(m_rows // tm,),
        in_specs=in_specs,
        out_specs=pl.BlockSpec((tm, d), lambda i: (i, 0)),
        out_shape=jax.ShapeDtypeStruct((m_rows, d), F32),
        compiler_params=_cparams(1),
        name=name,
    )(*args)


def _rope_tables(seq_len):
    n_freq = ATT_HD // 4
    inv_freq = (np.float32(ROPE_BASE) ** (-np.arange(n_freq, dtype=np.float32) / n_freq)).astype(np.float32)
    t = np.arange(seq_len)
    lane = np.arange(LANES)
    in_col_half = (lane % ATT_HD) >= ATT_HD // 2
    pos = np.where(in_col_half[None, :], (t % GRID_W)[:, None], (t // GRID_W)[:, None]).astype(np.float32)
    ang = pos * inv_freq[lane % n_freq][None, :]
    cos = np.cos(ang).astype(np.float32)
    sin = np.sin(ang).astype(np.float32)
    qs = np.float32(ATT_HD ** -0.5 * math.log2(math.e))
    perm = np.zeros((LANES, LANES), np.float32)
    for p in range(LANES):
        if p % (2 * n_freq) < n_freq:
            perm[p + n_freq, p] = -1.0
        else:
            perm[p - n_freq, p] = 1.0
    return (jnp.asarray(np.stack([cos * qs, cos])), jnp.asarray(np.stack([sin * qs, sin])),
            jnp.asarray(perm, dtype=BF16))


def kernel(x, c, ctx, c_ctx, w_mod, b_mod, norm_g, w_in, lambda_q1, lambda_k1, lambda_q2, lambda_k2,
           subln_g, w_pool, pool_scale, w_dw, b_dw, conv_ln_g, conv_ln_b, w_pw2, w_out, final_g):
    batch, seq, d = x.shape
    lc = ctx.shape[1]
    depth = w_mod.shape[0]
    att_w = d // 2
    pool_w = d // 4
    assert batch < 8 and seq % GRID_W == 0 and att_w == ATT_HEADS * ATT_VD and pool_w == 4 * POOL_GD
    assert seq % KV_CHUNK == 0 and lc % KV_CHUNK == 0

    cc = jnp.concatenate([c, c_ctx[None, :], jnp.zeros((8 - batch - 1, d), F32)], axis=0)
    lam_init = [0.8 - 0.6 * math.exp(-0.3 * l) for l in range(depth)]
    li = jnp.broadcast_to(jnp.asarray(lam_init, F32)[:, None, None], (depth, 1, LANES))
    mod, lam = _modulation(cc, w_mod, b_mod, lambda_q1, lambda_k1, lambda_q2, lambda_k2, li)
    tabs = _rope_tables(seq)

    tm_lat = min(1024, seq)
    tn = 512
    n_in = w_in.shape[-1]
    hcol = att_w // LANES
    pc_col0 = 4 * att_w
    x2d = x.reshape(batch * seq, d)
    c2d = ctx.reshape(batch * lc, d)
    row = lambda a: a.reshape(1, -1)

    for l in range(depth):
        last = l == depth - 1
        mod3 = mod[l].reshape(8, 1, 3 * d)
        lw = (w_pool[l].astype(BF16), row(pool_scale[l]), w_dw[l], row(b_dw[l]), row(conv_ln_g[l]),
              row(conv_ln_b[l]), w_pw2[l].astype(BF16))
        g_l = row(norm_g[l])
        out_scale = 1.0 - lam_init[l]

        proj = _inproj(x2d, mod3, g_l, w_in, tabs, layer=l, tm=tm_lat, tn=tn, tiles_per_mod=seq // tm_lat,
                       mod_row0=0, n0=0, n_tiles=n_in // tn, att_width=att_w, name=f"inproj_lat{l}")
        if last:
            cproj = _inproj(c2d, mod3, g_l, w_in, None, layer=l, tm=batch * lc, tn=tn, tiles_per_mod=1,
                            mod_row0=batch, n0=att_w // tn, n_tiles=2 * att_w // tn, att_width=att_w,
                            name=f"inproj_ctx{l}")
            kc_col, vc_col = 0, hcol
        else:
            cproj = _inproj(c2d, mod3, g_l, w_in, None, layer=l, tm=batch * lc, tn=tn, tiles_per_mod=1,
                            mod_row0=batch, n0=0, n_tiles=n_in // tn, att_width=att_w,
                            name=f"inproj_ctx{l}")
            kc_col, vc_col = hcol, 2 * hcol

        y_att = _attention(proj, cproj, proj, lam[l:l + 1], row(subln_g[l]), batch=batch, lq=seq,
                           heads=LAT_HEADS_PER_STEP, q_col=0, g_col=3 * hcol, lc=lc, kc_col=kc_col, vc_col=vc_col,
                           ll=seq, kl_col=hcol, vl_col=2 * hcol, out_scale=out_scale, name=f"attn_lat{l}")
        y_pool, y_conv = _poolconv(proj, *lw, seq_len=seq, rows=min(512, seq), col0=pc_col0, width=pool_w,
                                   name=f"poolconv_lat{l}")
        if not last:
            yc_att = _attention(cproj, cproj, None, lam[l:l + 1], row(subln_g[l]), batch=batch, lq=lc,
                                heads=ATT_HEADS, q_col=0, g_col=3 * hcol, lc=lc, kc_col=kc_col, vc_col=vc_col,
                                ll=0, kl_col=0, vl_col=0, out_scale=out_scale, name=f"attn_ctx{l}")
            yc_pool, yc_conv = _poolconv(cproj, *lw, seq_len=lc, rows=min(256, lc), col0=pc_col0,
                                         width=pool_w, name=f"poolconv_ctx{l}")
            c2d = _outproj(yc_att, yc_pool, yc_conv, w_out, c2d, mod3, None, layer=l, tm=min(512, batch * lc),
                           tiles_per_mod=batch * lc, mod_row0=batch, name=f"outproj_ctx{l}")
        tm_o = min(512, seq)
        x2d = _outproj(y_att, y_pool, y_conv, w_out, x2d, mod3, row(final_g) if last else None,
                       layer=l, tm=tm_o, tiles_per_mod=seq // tm_o, mod_row0=0, name=f"outproj_lat{l}")
    return x2d.reshape(batch, seq, d)
```

```python
import functools
import math

import jax
import jax.numpy as jnp
import numpy as np
from jax import lax
from jax.experimental import pallas as pl
from jax.experimental.pallas import tpu as pltpu

ATT_HEADS = 8
ATT_HD = 64
ATT_VD = 2 * ATT_HD
POOL_WINDOWS = (2, 4, 8, 16)
POOL_GD = 128
CONV_K = 31
GRID_W = 64
ROPE_BASE = 10000.0
EPS = 1e-6

LANES = 128
KV_CHUNK = 256
Q_SUB = 128
LAT_HEADS_PER_STEP = 4
SM_GROUP = 32
SM_AHEAD = 2
TICKS_PER_TRIP = 14
HALO = 16
VMEM_LIMIT = 56 * 1024 * 1024

F32 = jnp.float32
BF16 = jnp.bfloat16


def _cparams(n_axes, flags=None):
    return pltpu.CompilerParams(dimension_semantics=("arbitrary",) * n_axes,
                                vmem_limit_bytes=VMEM_LIMIT, flags=flags)


def _silu(v):
    return v * jax.nn.sigmoid(v)


def _mod_kernel(cc_ref, w_ref, b_ref, lq1_ref, lk1_ref, lq2_ref, lk2_ref, li_ref, out_ref, lam_ref):
    s = _silu(cc_ref[...])
    out_ref[0] = jnp.dot(s.astype(BF16), w_ref[0].astype(BF16), preferred_element_type=F32) + b_ref[0]
    a1 = jnp.sum(lq1_ref[0] * lk1_ref[0], axis=-1, keepdims=True)
    a2 = jnp.sum(lq2_ref[0] * lk2_ref[0], axis=-1, keepdims=True)
    lam = jnp.exp(a1) - jnp.exp(a2) + li_ref[0]
    lam_ref[0] = jnp.broadcast_to(lam, lam_ref.shape[1:])


def _modulation(cc, w_mod, b_mod, lq1, lk1, lq2, lk2, lam_init):
    depth, d, n = w_mod.shape
    tn = n // 4 if n % (4 * LANES) == 0 else n
    vec = lambda a: a.reshape(depth, 1, a.shape[-1])
    lspec = pl.BlockSpec((1, 1, ATT_HD), lambda l, j: (l, 0, 0))
    return pl.pallas_call(
        _mod_kernel,
        grid=(depth, n // tn),
        in_specs=[pl.BlockSpec((8, d), lambda l, j: (0, 0)),
                  pl.BlockSpec((1, d, tn), lambda l, j: (l, 0, j)),
                  pl.BlockSpec((1, 1, tn), lambda l, j: (l, 0, j)),
                  lspec, lspec, lspec, lspec,
                  pl.BlockSpec((1, 1, LANES), lambda l, j: (l, 0, 0))],
        out_specs=[pl.BlockSpec((1, 8, tn), lambda l, j: (l, 0, j)),
                   pl.BlockSpec((1, 8, LANES), lambda l, j: (l, 0, 0))],
        out_shape=[jax.ShapeDtypeStruct((depth, 8, n), F32),
                   jax.ShapeDtypeStruct((depth, 8, LANES), F32)],
        compiler_params=_cparams(2),
        name="modulation",
    )(cc, w_mod, vec(b_mod), vec(lq1), vec(lk1), vec(lq2), vec(lk2), lam_init)


def _inproj_kernel(*refs, d_model, rope, n_rope_tiles, n_q_tiles, row_chunk):
    if rope:
        x_ref, mod_ref, g_ref, w_ref, cos_ref, sin_ref, perm_ref, out_ref, h_ref = refs
    else:
        x_ref, mod_ref, g_ref, w_ref, out_ref, h_ref = refs
    j = pl.program_id(1)
    tm, tn = out_ref.shape

    def rope_store(rows, acc):
        cos = cos_ref[rows, :]
        sin = sin_ref[rows, :]
        for hh in range(tn // LANES):
            t = acc[:, hh * LANES:(hh + 1) * LANES]
            rot = jnp.dot(t.astype(BF16), perm_ref[...], preferred_element_type=F32)
            out_ref[rows, hh * LANES:(hh + 1) * LANES] = (t * cos + rot * sin).astype(out_ref.dtype)

    def plain_store(rows, acc):
        qs = jnp.where(j < n_q_tiles, ATT_HD ** -0.5 * math.log2(math.e), 1.0).astype(F32)
        out_ref[rows, :] = (acc * qs).astype(out_ref.dtype)

    @pl.when(j == 0)
    def _():
        m = mod_ref[0]
        shift = m[:, :d_model]
        gain = g_ref[...] * (1.0 + m[:, d_model:2 * d_model])
        wb = w_ref[...].astype(BF16)
        for c in range(tm // row_chunk):
            rows = slice(c * row_chunk, (c + 1) * row_chunk)
            xf = x_ref[rows, :]
            ms = jnp.mean(xf * xf, axis=-1, keepdims=True)
            h = (xf * lax.rsqrt(ms + EPS) * gain + shift).astype(BF16)
            h_ref[rows, :] = h
            acc = jnp.dot(h, wb, preferred_element_type=F32)
            if rope:
                rope_store(rows, acc)
            else:
                plain_store(rows, acc)

    if rope and n_rope_tiles > 1:
        @pl.when(jnp.logical_and(j > 0, j < n_rope_tiles))
        def _():
            rope_store(slice(None), jnp.dot(h_ref[...], w_ref[...].astype(BF16), preferred_element_type=F32))

    @pl.when(j >= (n_rope_tiles if rope else 1))
    def _():
        plain_store(slice(None), jnp.dot(h_ref[...], w_ref[...].astype(BF16), preferred_element_type=F32))


def _inproj(x2d, mod3, g, w, tabs, *, layer, tm, tn, tiles_per_mod, mod_row0, n0, n_tiles, att_width, name):
    m_rows, d = x2d.shape
    rope = tabs is not None
    n_q_tiles = max(att_width // tn - n0, 0)
    in_specs = [pl.BlockSpec((tm, d), lambda i, j: (i, 0)),
                pl.BlockSpec((1, 1, mod3.shape[-1]), lambda i, j: (mod_row0 + i // tiles_per_mod, 0, 0)),
                pl.BlockSpec((1, d), lambda i, j: (0, 0)),
                pl.BlockSpec((None, d, tn), lambda i, j: (layer, 0, n0 + j))]
    args = [x2d, mod3, g, w]
    if rope:
        tiles_per_seq = tabs[0].shape[1] // tm
        tspec = pl.BlockSpec((None, tm, LANES),
                             lambda i, j: (jnp.where(j < n_q_tiles, 0, 1), i % tiles_per_seq, 0))
        in_specs += [tspec, tspec, pl.BlockSpec((LANES, LANES), lambda i, j: (0, 0))]
        args += list(tabs)
    kern = functools.partial(_inproj_kernel, d_model=d, rope=rope,
                             n_rope_tiles=max(2 * att_width // tn - n0, 0), n_q_tiles=n_q_tiles,
                             row_chunk=min(256, tm))
    return pl.pallas_call(
        kern,
        grid=(m_rows // tm, n_tiles),
        in_specs=in_specs,
        out_specs=pl.BlockSpec((tm, tn), lambda i, j: (i, j)),
        out_shape=jax.ShapeDtypeStruct((m_rows, n_tiles * tn), BF16),
        scratch_shapes=[pltpu.VMEM((tm, d), BF16)],
        compiler_params=_cparams(2),
        name=name,
    )(*args)


def _attn_kernel(*refs, lc, ll, n_sub, heads, out_scale):
    if ll:
        q_ref, kc_ref, vc_ref, kl_ref, vl_ref, g_ref, lam_ref, sg_ref, out_ref, vt_ref, *slots = refs
        k_parts = [(kc_ref, 0, lc), (kl_ref, lc, ll)]
        v_parts = [(vc_ref, 0, lc), (vl_ref, lc, ll)]
    else:
        q_ref, kc_ref, vc_ref, g_ref, lam_ref, sg_ref, out_ref, vt_ref, *slots = refs
        k_parts = [(kc_ref, 0, lc)]
        v_parts = [(vc_ref, 0, lc)]
    lk = lc + ll
    st_ref = slots[0:2]
    pt_ref = slots[2:4]

    def head_lanes(hh):
        return slice(hh * LANES, (hh + 1) * LANES)

    for hh in range(heads):
        for ref, off, n in v_parts:
            for c in range(n // KV_CHUNK):
                blk = ref[c * KV_CHUNK:(c + 1) * KV_CHUNK, head_lanes(hh)].astype(F32)
                vt_ref[hh, :, off + c * KV_CHUNK:off + (c + 1) * KV_CHUNK] = blk.T.astype(BF16)

    row = lax.broadcasted_iota(jnp.int32, (ATT_VD, Q_SUB), 0)
    lam = lam_ref[0][0:1, :]
    sg = sg_ref[...]

    def item(t):
        if isinstance(t, int):
            return t // n_sub, pl.ds((t % n_sub) * Q_SUB, Q_SUB)
        assert heads == 1
        return 0, pl.ds(pl.multiple_of(t * Q_SUB, Q_SUB), Q_SUB)

    def scores(t, slot):
        hh, q_rows = item(t)
        qt = q_ref[q_rows, head_lanes(hh)].astype(F32).T
        qbd = jnp.concatenate([jnp.where(row < ATT_HD, qt, 0.0),
                               jnp.where(row >= ATT_HD, qt, 0.0)], axis=1).astype(BF16)
        m8 = None
        for k_ref, off, n in k_parts:
            for c in range(n // KV_CHUNK):
                kch = k_ref[c * KV_CHUNK:(c + 1) * KV_CHUNK, head_lanes(hh)]
                st = jnp.dot(kch, qbd, preferred_element_type=F32)
                st_ref[slot][off + c * KV_CHUNK:off + (c + 1) * KV_CHUNK, :] = st
                for r in range(KV_CHUNK // 8):
                    piece = st[r * 8:(r + 1) * 8, :]
                    m8 = piece if m8 is None else jnp.maximum(m8, piece)
        return jnp.max(m8, axis=0, keepdims=True)

    def softmax(slot, m):
        l8 = None
        marks = []
        m8 = jnp.broadcast_to(m, (8, 2 * Q_SUB))
        for g in range(lk // SM_GROUP):
            rows = slice(g * SM_GROUP, (g + 1) * SM_GROUP)
            m_use = m8 if g < SM_AHEAD else m8 + marks[g - SM_AHEAD] * 0.0
            p = jnp.exp2(st_ref[slot][rows, :] - jnp.tile(m_use, (SM_GROUP // 8, 1)))
            for r in range(SM_GROUP // 8):
                piece = p[r * 8:(r + 1) * 8, :]
                l8 = piece if l8 is None else l8 + piece
            marks.append(l8)
            pt_ref[slot][rows, :] = p.astype(BF16)
        return l8

    def pv(t, slot, l8):
        hh, q_rows = item(t)
        acc = jnp.dot(vt_ref[hh], pt_ref[slot][...], preferred_element_type=F32)
        o = acc * (1.0 / jnp.sum(l8, axis=0, keepdims=True))
        ot = o[:, :Q_SUB] - lam * o[:, Q_SUB:]
        oq = ot.T
        ms = jnp.mean(oq * oq, axis=-1, keepdims=True)
        y = oq * lax.rsqrt(ms + EPS) * sg * out_scale
        gate = g_ref[q_rows, head_lanes(hh)].astype(F32)
        out_ref[q_rows, head_lanes(hh)] = (y * _silu(gate)).astype(out_ref.dtype)

    def tick(t, slot, carry, do_scores=True, do_softmax=True, do_pv=True):
        m_prev, l8_prev = carry
        m = scores(t, slot) if do_scores else None
        l8 = softmax(1 - slot, m_prev) if do_softmax else None
        if do_pv:
            pv(t - 2, slot, l8_prev)
        return m, l8

    n_items = heads * n_sub
    n_trips = (n_items - 2) // TICKS_PER_TRIP if heads == 1 and n_items - 2 >= 2 * TICKS_PER_TRIP else 0
    n_loop = n_trips * TICKS_PER_TRIP
    carry = (None, None)
    t = 0
    while t < 2:
        carry = tick(t, t % 2, carry, do_scores=t < n_items, do_softmax=1 <= t <= n_items, do_pv=False)
        t += 1
    if n_loop:
        def trip(j, c):
            t0 = 2 + TICKS_PER_TRIP * j
            for u in range(TICKS_PER_TRIP):
                c = tick(t0 + u, u % 2, c)
            return c
        carry = lax.fori_loop(0, n_trips, trip, carry)
        t += n_loop
    while t < n_items + 2:
        carry = tick(t, t % 2, carry, do_scores=t < n_items, do_softmax=1 <= t <= n_items, do_pv=True)
        t += 1


def _attention(q_arr, kv_ctx, kv_lat, lam, subln_g, *, batch, lq, heads, q_col, g_col, lc, kc_col, vc_col,
               ll, kl_col, vl_col, out_scale, name):
    width = heads * LANES
    assert ATT_HEADS % heads == 0 and all(c % heads == 0 for c in (q_col, g_col, kc_col, vc_col, kl_col, vl_col))
    col = lambda c: (lambda b, h: (b, c // heads + h))
    in_specs = [pl.BlockSpec((lq, width), col(q_col)),
                pl.BlockSpec((lc, width), col(kc_col)),
                pl.BlockSpec((lc, width), col(vc_col))]
    args = [q_arr, kv_ctx, kv_ctx]
    if kv_lat is not None:
        in_specs += [pl.BlockSpec((ll, width), col(kl_col)),
                     pl.BlockSpec((ll, width), col(vl_col))]
        args += [kv_lat, kv_lat]
    else:
        ll = 0
    in_specs += [pl.BlockSpec((lq, width), col(g_col)),
                 pl.BlockSpec((1, 8, LANES), lambda b, h: (0, 0, 0)),
                 pl.BlockSpec((1, ATT_VD), lambda b, h: (0, 0))]
    args += [q_arr, lam, subln_g]
    kern = functools.partial(_attn_kernel, lc=lc, ll=ll, n_sub=lq // Q_SUB, heads=heads, out_scale=out_scale)
    return pl.pallas_call(
        kern,
        grid=(batch, ATT_HEADS // heads),
        in_specs=in_specs,
        out_specs=pl.BlockSpec((lq, width), lambda b, h: (b, h)),
        out_shape=jax.ShapeDtypeStruct((batch * lq, ATT_HEADS * ATT_VD), BF16),
        scratch_shapes=[pltpu.VMEM((heads, ATT_VD, lc + ll), BF16),
                        pltpu.VMEM((lc + ll, 2 * Q_SUB), F32), pltpu.VMEM((lc + ll, 2 * Q_SUB), F32),
                        pltpu.VMEM((lc + ll, 2 * Q_SUB), BF16), pltpu.VMEM((lc + ll, 2 * Q_SUB), BF16)],
        compiler_params=_cparams(2),
        name=name,
    )(*args)


def _poolconv_kernel(up_c, up_p, up_n, gp_ref, a_c, a_p, a_n, b_c, b_p, b_n, gc_ref,
                     wpool_ref, pscale_ref, wdw_ref, bdw_ref, lng_ref, lnb_ref, wpw_ref,
                     ypool_ref, yconv_ref, u_ref, p_ref, z_ref, s_ref, r_ref, *, rows, tiles_per_seq, seq_len,
                     row_blk):
    iseq = pl.program_id(0) % tiles_per_seq
    has_prev = iseq > 0
    has_next = iseq < tiles_per_seq - 1

    def glu(a_ref, b_ref):
        return a_ref[...].astype(F32) * jax.nn.sigmoid(b_ref[...].astype(F32))

    u_ref[0:HALO, :] = jnp.where(has_prev, glu(a_p, b_p), 0.0)
    u_ref[HALO:HALO + rows, :] = glu(a_c, b_c)
    u_ref[HALO + rows:, :] = jnp.where(has_next, glu(a_n, b_n), 0.0)
    p_ref[0:HALO, :] = jnp.where(has_prev, up_p[...].astype(F32), 0.0)
    p_ref[HALO:HALO + rows, :] = up_c[...].astype(F32)
    p_ref[HALO + rows:, :] = jnp.where(has_next, up_n[...].astype(F32), 0.0)

    ext = rows + 2 * HALO - 8
    for b in range(1, 8):
        s_ref[b - 1] = u_ref[b:b + ext, :]

    for rb in range(rows // row_blk):
        acc = None
        for j in range(CONV_K):
            off = HALO - CONV_K // 2 + j
            base = rb * row_blk + (off // 8) * 8
            if off % 8 == 0:
                tap = u_ref[base:base + row_blk, :]
            else:
                tap = s_ref[off % 8 - 1, base:base + row_blk, :]
            term = tap * wdw_ref[j:j + 1, :]
            acc = term if acc is None else acc + term
        y = acc + bdw_ref[...]
        mu = jnp.mean(y, axis=-1, keepdims=True)
        yc = y - mu
        var = jnp.mean(yc * yc, axis=-1, keepdims=True)
        z = yc * lax.rsqrt(var + EPS) * lng_ref[...] + lnb_ref[...]
        z_ref[rb * row_blk:(rb + 1) * row_blk, :] = _silu(z).astype(BF16)
    yconv = jnp.dot(z_ref[...], wpw_ref[...], preferred_element_type=F32)
    yconv_ref[...] = (yconv * _silu(gc_ref[...].astype(F32))).astype(yconv_ref.dtype)

    t = iseq * rows + lax.broadcasted_iota(jnp.int32, (rows, POOL_GD), 0)
    for g, w in enumerate(POOL_WINDOWS):
        cols = slice(g * POOL_GD, (g + 1) * POOL_GD)
        hw = w // 2
        load = lambda s, m, _c=cols: p_ref[s:s + m, _c]
        n, level = 1, 0
        while n < hw:
            dst = r_ref.at[level % 2]
            dst[8:rows + HALO + 8, :] = load(8, rows + HALO) + load(8 + n, rows + HALO)
            dst[rows + HALO + 8:, :] = jnp.zeros((8, POOL_GD), F32)
            load = lambda s, m, _d=dst: _d[s:s + m, :]
            n, level = 2 * n, level + 1
        ssum = load(HALO - hw, rows) + load(HALO, rows)
        cnt = (jnp.minimum(t + hw, seq_len) - jnp.maximum(t - hw, 0)).astype(F32)
        dlt = ssum / cnt - p_ref[HALO:HALO + rows, cols]
        yp = jnp.dot(dlt.astype(BF16), wpool_ref[g], preferred_element_type=F32) * pscale_ref[:, cols]
        ypool_ref[:, cols] = (yp * _silu(gp_ref[:, cols].astype(F32))).astype(ypool_ref.dtype)


def _poolconv(proj, wpool, pscale, wdw, bdw, lng, lnb, wpw, *, seq_len, rows, col0, width, name):
    m_rows = proj.shape[0]
    tiles_per_seq = seq_len // rows
    hpr = rows // HALO
    n_halo = m_rows // HALO
    cb = col0 // width

    def cur(k):
        return pl.BlockSpec((rows, width), lambda i: (i, cb + k))

    def prev(k):
        return pl.BlockSpec((HALO, width), lambda i: (jnp.maximum(i * hpr - 1, 0), cb + k))

    def nxt(k):
        return pl.BlockSpec((HALO, width), lambda i: (jnp.minimum((i + 1) * hpr, n_halo - 1), cb + k))

    full = lambda a: pl.BlockSpec(a.shape, lambda i: (0,) * a.ndim)
    consts = [wpool, pscale, wdw, bdw, lng, lnb, wpw]
    kern = functools.partial(_poolconv_kernel, rows=rows, tiles_per_seq=tiles_per_seq, seq_len=seq_len,
                             row_blk=min(64, rows))
    return pl.pallas_call(
        kern,
        grid=(m_rows // rows,),
        in_specs=[cur(0), prev(0), nxt(0), cur(1), cur(2), prev(2), nxt(2), cur(3), prev(3), nxt(3), cur(4)]
                 + [full(a) for a in consts],
        out_specs=[pl.BlockSpec((rows, width), lambda i: (i, 0)),
                   pl.BlockSpec((rows, width), lambda i: (i, 0))],
        out_shape=[jax.ShapeDtypeStruct((m_rows, width), BF16),
                   jax.ShapeDtypeStruct((m_rows, width), BF16)],
        scratch_shapes=[pltpu.VMEM((rows + 2 * HALO, width), F32),
                        pltpu.VMEM((rows + 2 * HALO, width), F32),
                        pltpu.VMEM((rows, width), BF16),
                        pltpu.VMEM((7, rows + 2 * HALO - 8, width), F32),
                        pltpu.VMEM((2, rows + 2 * HALO, POOL_GD), F32)],
        compiler_params=_cparams(1),
        name=name,
    )(*([proj] * 11), *consts)


def _outproj_kernel(*refs, d_model, att_width, pool_width, final):
    if final:
        ya_ref, yp_ref, yc_ref, w_ref, x_ref, mod_ref, fg_ref, out_ref = refs
    else:
        ya_ref, yp_ref, yc_ref, w_ref, x_ref, mod_ref, out_ref = refs
    c1 = att_width
    c2 = att_width + pool_width
    y = jnp.dot(ya_ref[...], w_ref[0:c1, :].astype(BF16), preferred_element_type=F32)
    y = y + jnp.dot(yp_ref[...], w_ref[c1:c2, :].astype(BF16), preferred_element_type=F32)
    y = y + jnp.dot(yc_ref[...], w_ref[c2:, :].astype(BF16), preferred_element_type=F32)
    gate = mod_ref[0][:, 2 * d_model:]
    xn = x_ref[...] + gate * y
    if final:
        ms = jnp.mean(xn * xn, axis=-1, keepdims=True)
        xn = xn * lax.rsqrt(ms + EPS) * fg_ref[...]
    out_ref[...] = xn


def _outproj(ya, yp, yc, w, x2d, mod3, final_g, *, layer, tm, tiles_per_mod, mod_row0, name):
    m_rows, d = x2d.shape
    final = final_g is not None
    in_specs = [pl.BlockSpec((tm, ya.shape[1]), lambda i: (i, 0)),
                pl.BlockSpec((tm, yp.shape[1]), lambda i: (i, 0)),
                pl.BlockSpec((tm, yc.shape[1]), lambda i: (i, 0)),
                pl.BlockSpec((None,) + w.shape[1:], lambda i: (layer, 0, 0), pipeline_mode=pl.Buffered(1)),
                pl.BlockSpec((tm, d), lambda i: (i, 0)),
                pl.BlockSpec((1, 1, mod3.shape[-1]), lambda i: (mod_row0 + i // tiles_per_mod, 0, 0))]
    args = [ya, yp, yc, w, x2d, mod3]
    if final:
        in_specs.append(pl.BlockSpec((1, d), lambda i: (0, 0)))
        args.append(final_g)
    kern = functools.partial(_outproj_kernel, d_model=d, att_width=ya.shape[1], pool_width=yp.shape[1],
                             final=final)
    return pl.pallas_call(
        kern,
        grid=(m_rows // tm,),
        in_specs=in_specs,
        out_specs=pl.BlockSpec((tm, d), lambda i: (i, 0)),
        out_shape=jax.ShapeDtypeStruct((m_rows, d), F32),
        compiler_params=_cparams(1),
        name=name,
    )(*args)


def _rope_tables(seq_len):
    n_freq = ATT_HD // 4
    inv_freq = (np.float32(ROPE_BASE) ** (-np.arange(n_freq, dtype=np.float32) / n_freq)).astype(np.float32)
    t = np.arange(seq_len)
    lane = np.arange(LANES)
    in_col_half = (lane % ATT_HD) >= ATT_HD // 2
    pos = np.where(in_col_half[None, :], (t % GRID_W)[:, None], (t // GRID_W)[:, None]).astype(np.float32)
    ang = pos * inv_freq[lane % n_freq][None, :]
    cos = np.cos(ang).astype(np.float32)
    sin = np.sin(ang).astype(np.float32)
    qs = np.float32(ATT_HD ** -0.5 * math.log2(math.e))
    perm = np.zeros((LANES, LANES), np.float32)
    for p in range(LANES):
        if p % (2 * n_freq) < n_freq:
            perm[p + n_freq, p] = -1.0
        else:
            perm[p - n_freq, p] = 1.0
    return (jnp.asarray(np.stack([cos * qs, cos])), jnp.asarray(np.stack([sin * qs, sin])),
            jnp.asarray(perm, dtype=BF16))


def kernel(x, c, ctx, c_ctx, w_mod, b_mod, norm_g, w_in, lambda_q1, lambda_k1, lambda_q2, lambda_k2,
           subln_g, w_pool, pool_scale, w_dw, b_dw, conv_ln_g, conv_ln_b, w_pw2, w_out, final_g):
    batch, seq, d = x.shape
    lc = ctx.shape[1]
    depth = w_mod.shape[0]
    att_w = d // 2
    pool_w = d // 4
    assert batch < 8 and seq % GRID_W == 0 and att_w == ATT_HEADS * ATT_VD and pool_w == 4 * POOL_GD
    assert seq % KV_CHUNK == 0 and lc % KV_CHUNK == 0

    cc = jnp.concatenate([c, c_ctx[None, :], jnp.zeros((8 - batch - 1, d), F32)], axis=0)
    lam_init = [0.8 - 0.6 * math.exp(-0.3 * l) for l in range(depth)]
    li = jnp.broadcast_to(jnp.asarray(lam_init, F32)[:, None, None], (depth, 1, LANES))
    mod, lam = _modulation(cc, w_mod, b_mod, lambda_q1, lambda_k1, lambda_q2, lambda_k2, li)
    tabs = _rope_tables(seq)

    tm_lat = min(1024, seq)
    tn = 512
    n_in = w_in.shape[-1]
    hcol = att_w // LANES
    pc_col0 = 4 * att_w
    x2d = x.reshape(batch * seq, d)
    c2d = ctx.reshape(batch * lc, d)
    row = lambda a: a.reshape(1, -1)

    for l in range(depth):
        last = l == depth - 1
        mod3 = mod[l].reshape(8, 1, 3 * d)
        lw = (w_pool[l].astype(BF16), row(pool_scale[l]), w_dw[l], row(b_dw[l]), row(conv_ln_g[l]),
              row(conv_ln_b[l]), w_pw2[l].astype(BF16))
        g_l = row(norm_g[l])
        out_scale = 1.0 - lam_init[l]

        proj = _inproj(x2d, mod3, g_l, w_in, tabs, layer=l, tm=tm_lat, tn=tn, tiles_per_mod=seq // tm_lat,
                       mod_row0=0, n0=0, n_tiles=n_in // tn, att_width=att_w, name=f"inproj_lat{l}")
        if last:
            cproj = _inproj(c2d, mod3, g_l, w_in, None, layer=l, tm=batch * lc, tn=tn, tiles_per_mod=1,
                            mod_row0=batch, n0=att_w // tn, n_tiles=2 * att_w // tn, att_width=att_w,
                            name=f"inproj_ctx{l}")
            kc_col, vc_col = 0, hcol
        else:
            cproj = _inproj(c2d, mod3, g_l, w_in, None, layer=l, tm=batch * lc, tn=tn, tiles_per_mod=1,
                            mod_row0=batch, n0=0, n_tiles=n_in // tn, att_width=att_w,
                            name=f"inproj_ctx{l}")
            kc_col, vc_col = hcol, 2 * hcol

        y_att = _attention(proj, cproj, proj, lam[l:l + 1], row(subln_g[l]), batch=batch, lq=seq,
                           heads=LAT_HEADS_PER_STEP, q_col=0, g_col=3 * hcol, lc=lc, kc_col=kc_col, vc_col=vc_col,
                           ll=seq, kl_col=hcol, vl_col=2 * hcol, out_scale=out_scale, name=f"attn_lat{l}")
        y_pool, y_conv = _poolconv(proj, *lw, seq_len=seq, rows=min(512, seq), col0=pc_col0, width=pool_w,
                                   name=f"poolconv_lat{l}")
        if not last:
            yc_att = _attention(cproj, cproj, None, lam[l:l + 1], row(subln_g[l]), batch=batch, lq=lc,
                                heads=ATT_HEADS, q_col=0, g_col=3 * hcol, lc=lc, kc_col=kc_col, vc_col=vc_col,
                                ll=0, kl_col=0, vl_col=0, out_scale=out_scale, name=f"attn_ctx{l}")
            yc_pool, yc_conv = _poolconv(cproj, *lw, seq_len=lc, rows=min(256, lc), col0=pc_col0,
                                         width=pool_w, name=f"poolconv_ctx{l}")
            c2d = _outproj(yc_att, yc_pool, yc_conv, w_out, c2d, mod3, None, layer=l, tm=min(512, batch * lc),
                           tiles_per_mod=batch * lc, mod_row0=batch, name=f"outproj_ctx{l}")
        tm_o = min(512, seq)
        x2d = _outproj(y_att, y_pool, y_conv, w_out, x2d, mod3, row(final_g) if last else None,
                       layer=l, tm=tm_o, tiles_per_mod=seq // tm_o, mod_row0=0, name=f"outproj_lat{l}")
    return x2d.reshape(batch, seq, d)
```

```python
import functools
import math

import jax
import jax.numpy as jnp
import numpy as np
from jax import lax
from jax.experimental import pallas as pl
from jax.experimental.pallas import tpu as pltpu

ATT_HEADS = 8
ATT_HD = 64
ATT_VD = 2 * ATT_HD
POOL_WINDOWS = (2, 4, 8, 16)
POOL_GD = 128
CONV_K = 31
GRID_W = 64
ROPE_BASE = 10000.0
EPS = 1e-6

LANES = 128
KV_CHUNK = 256
Q_SUB = 128
LAT_HEADS_PER_STEP = 2
SM_GROUP = 32
SM_AHEAD = 2
TICKS_PER_TRIP = 14
HALO = 16
VMEM_LIMIT = 56 * 1024 * 1024

F32 = jnp.float32
BF16 = jnp.bfloat16


def _cparams(n_axes, flags=None):
    return pltpu.CompilerParams(dimension_semantics=("arbitrary",) * n_axes,
                                vmem_limit_bytes=VMEM_LIMIT, flags=flags)


def _silu(v):
    return v * jax.nn.sigmoid(v)


def _mod_kernel(cc_ref, w_ref, b_ref, lq1_ref, lk1_ref, lq2_ref, lk2_ref, li_ref, out_ref, lam_ref):
    s = _silu(cc_ref[...])
    out_ref[0] = jnp.dot(s.astype(BF16), w_ref[0].astype(BF16), preferred_element_type=F32) + b_ref[0]
    a1 = jnp.sum(lq1_ref[0] * lk1_ref[0], axis=-1, keepdims=True)
    a2 = jnp.sum(lq2_ref[0] * lk2_ref[0], axis=-1, keepdims=True)
    lam = jnp.exp(a1) - jnp.exp(a2) + li_ref[0]
    lam_ref[0] = jnp.broadcast_to(lam, lam_ref.shape[1:])


def _modulation(cc, w_mod, b_mod, lq1, lk1, lq2, lk2, lam_init):
    depth, d, n = w_mod.shape
    tn = n // 4 if n % (4 * LANES) == 0 else n
    vec = lambda a: a.reshape(depth, 1, a.shape[-1])
    lspec = pl.BlockSpec((1, 1, ATT_HD), lambda l, j: (l, 0, 0))
    return pl.pallas_call(
        _mod_kernel,
        grid=(depth, n // tn),
        in_specs=[pl.BlockSpec((8, d), lambda l, j: (0, 0)),
                  pl.BlockSpec((1, d, tn), lambda l, j: (l, 0, j)),
                  pl.BlockSpec((1, 1, tn), lambda l, j: (l, 0, j)),
                  lspec, lspec, lspec, lspec,
                  pl.BlockSpec((1, 1, LANES), lambda l, j: (l, 0, 0))],
        out_specs=[pl.BlockSpec((1, 8, tn), lambda l, j: (l, 0, j)),
                   pl.BlockSpec((1, 8, LANES), lambda l, j: (l, 0, 0))],
        out_shape=[jax.ShapeDtypeStruct((depth, 8, n), F32),
                   jax.ShapeDtypeStruct((depth, 8, LANES), F32)],
        compiler_params=_cparams(2),
        name="modulation",
    )(cc, w_mod, vec(b_mod), vec(lq1), vec(lk1), vec(lq2), vec(lk2), lam_init)


def _inproj_kernel(*refs, d_model, rope, n_rope_tiles, n_q_tiles, row_chunk):
    if rope:
        x_ref, mod_ref, g_ref, w_ref, cos_ref, sin_ref, perm_ref, out_ref, h_ref = refs
    else:
        x_ref, mod_ref, g_ref, w_ref, out_ref, h_ref = refs
    j = pl.program_id(1)
    tm, tn = out_ref.shape
    row0 = pl.multiple_of(pl.program_id(2) * tm, tm)

    def rope_store(rows, acc):
        cos = cos_ref[rows, :]
        sin = sin_ref[rows, :]
        for hh in range(tn // LANES):
            t = acc[:, hh * LANES:(hh + 1) * LANES]
            rot = jnp.dot(t.astype(BF16), perm_ref[...], preferred_element_type=F32)
            out_ref[rows, hh * LANES:(hh + 1) * LANES] = (t * cos + rot * sin).astype(out_ref.dtype)

    def plain_store(rows, acc):
        qs = jnp.where(j < n_q_tiles, ATT_HD ** -0.5 * math.log2(math.e), 1.0).astype(F32)
        out_ref[rows, :] = (acc * qs).astype(out_ref.dtype)

    @pl.when(j == 0)
    def _():
        m = mod_ref[0]
        shift = m[:, :d_model]
        gain = g_ref[...] * (1.0 + m[:, d_model:2 * d_model])
        wb = w_ref[...].astype(BF16)
        for c in range(tm // row_chunk):
            rows = slice(c * row_chunk, (c + 1) * row_chunk)
            xf = x_ref[rows, :]
            ms = jnp.mean(xf * xf, axis=-1, keepdims=True)
            h = (xf * lax.rsqrt(ms + EPS) * gain + shift).astype(BF16)
            h_ref[pl.ds(row0 + c * row_chunk, row_chunk), :] = h
            acc = jnp.dot(h, wb, preferred_element_type=F32)
            if rope:
                rope_store(rows, acc)
            else:
                plain_store(rows, acc)

    def later_tile():
        return jnp.dot(h_ref[pl.ds(row0, tm), :], w_ref[...].astype(BF16), preferred_element_type=F32)

    if rope and n_rope_tiles > 1:
        @pl.when(jnp.logical_and(j > 0, j < n_rope_tiles))
        def _():
            rope_store(slice(None), later_tile())

    @pl.when(j >= (n_rope_tiles if rope else 1))
    def _():
        plain_store(slice(None), later_tile())


def _inproj(x2d, mod3, g, w, tabs, *, layer, tm, tn, group, tiles_per_mod, mod_row0, n0, n_tiles, att_width,
            name):
    m_rows, d = x2d.shape
    rope = tabs is not None
    n_q_tiles = max(att_width // tn - n0, 0)
    n_rope_tiles = max(2 * att_width // tn - n0, 0)
    tile = lambda gi, i: gi * group + i
    in_specs = [pl.BlockSpec((tm, d), lambda gi, j, i: (jnp.where(j == 0, tile(gi, i), tile(gi, group - 1)), 0)),
                pl.BlockSpec((1, 1, mod3.shape[-1]),
                             lambda gi, j, i: (mod_row0 + tile(gi, i) // tiles_per_mod, 0, 0)),
                pl.BlockSpec((1, d), lambda gi, j, i: (0, 0)),
                pl.BlockSpec((None, d, tn), lambda gi, j, i: (layer, 0, n0 + j))]
    args = [x2d, mod3, g, w]
    if rope:
        tiles_per_seq = tabs[0].shape[1] // tm
        tspec = pl.BlockSpec((None, tm, LANES),
                             lambda gi, j, i: (jnp.where(j < n_q_tiles, 0, 1),
                                               jnp.where(j < n_rope_tiles, tile(gi, i) % tiles_per_seq, 0), 0))
        in_specs += [tspec, tspec, pl.BlockSpec((LANES, LANES), lambda gi, j, i: (0, 0))]
        args += list(tabs)
    kern = functools.partial(_inproj_kernel, d_model=d, rope=rope, n_rope_tiles=n_rope_tiles,
                             n_q_tiles=n_q_tiles, row_chunk=min(256, tm))
    return pl.pallas_call(
        kern,
        grid=(m_rows // (tm * group), n_tiles, group),
        in_specs=in_specs,
        out_specs=pl.BlockSpec((tm, tn), lambda gi, j, i: (tile(gi, i), j)),
        out_shape=jax.ShapeDtypeStruct((m_rows, n_tiles * tn), BF16),
        scratch_shapes=[pltpu.VMEM((group * tm, d), BF16)],
        compiler_params=_cparams(3),
        name=name,
    )(*args)


def _attn_kernel(*refs, lc, ll, n_sub, heads, out_scale):
    if ll:
        q_ref, kc_ref, vc_ref, kl_ref, vl_ref, g_ref, lam_ref, sg_ref, out_ref, vt_ref, *slots = refs
        k_parts = [(kc_ref, 0, lc), (kl_ref, lc, ll)]
        v_parts = [(vc_ref, 0, lc), (vl_ref, lc, ll)]
    else:
        q_ref, kc_ref, vc_ref, g_ref, lam_ref, sg_ref, out_ref, vt_ref, *slots = refs
        k_parts = [(kc_ref, 0, lc)]
        v_parts = [(vc_ref, 0, lc)]
    lk = lc + ll
    st_ref = slots[0:2]
    pt_ref = slots[2:4]

    def head_lanes(hh):
        return slice(hh * LANES, (hh + 1) * LANES)

    for hh in range(heads):
        for ref, off, n in v_parts:
            for c in range(n // KV_CHUNK):
                blk = ref[c * KV_CHUNK:(c + 1) * KV_CHUNK, head_lanes(hh)].astype(F32)
                vt_ref[hh, :, off + c * KV_CHUNK:off + (c + 1) * KV_CHUNK] = blk.T.astype(BF16)

    row = lax.broadcasted_iota(jnp.int32, (ATT_VD, Q_SUB), 0)
    lam = lam_ref[0][0:1, :]
    sg = sg_ref[...]

    def item(t):
        if isinstance(t, int):
            return t // n_sub, pl.ds((t % n_sub) * Q_SUB, Q_SUB)
        assert heads == 1
        return 0, pl.ds(pl.multiple_of(t * Q_SUB, Q_SUB), Q_SUB)

    def scores(t, slot):
        hh, q_rows = item(t)
        qt = q_ref[q_rows, head_lanes(hh)].astype(F32).T
        qbd = jnp.concatenate([jnp.where(row < ATT_HD, qt, 0.0),
                               jnp.where(row >= ATT_HD, qt, 0.0)], axis=1).astype(BF16)
        m8 = None
        for k_ref, off, n in k_parts:
            for c in range(n // KV_CHUNK):
                kch = k_ref[c * KV_CHUNK:(c + 1) * KV_CHUNK, head_lanes(hh)]
                st = jnp.dot(kch, qbd, preferred_element_type=F32)
                st_ref[slot][off + c * KV_CHUNK:off + (c + 1) * KV_CHUNK, :] = st
                for r in range(KV_CHUNK // 8):
                    piece = st[r * 8:(r + 1) * 8, :]
                    m8 = piece if m8 is None else jnp.maximum(m8, piece)
        return jnp.max(m8, axis=0, keepdims=True)

    def softmax(slot, m):
        l8 = None
        marks = []
        m8 = jnp.broadcast_to(m, (8, 2 * Q_SUB))
        for g in range(lk // SM_GROUP):
            rows = slice(g * SM_GROUP, (g + 1) * SM_GROUP)
            m_use = m8 if g < SM_AHEAD else m8 + marks[g - SM_AHEAD] * 0.0
            p = jnp.exp2(st_ref[slot][rows, :] - jnp.tile(m_use, (SM_GROUP // 8, 1)))
            for r in range(SM_GROUP // 8):
                piece = p[r * 8:(r + 1) * 8, :]
                l8 = piece if l8 is None else l8 + piece
            marks.append(l8)
            pt_ref[slot][rows, :] = p.astype(BF16)
        return l8

    def pv(t, slot, l8):
        hh, q_rows = item(t)
        acc = jnp.dot(vt_ref[hh], pt_ref[slot][...], preferred_element_type=F32)
        o = acc * (1.0 / jnp.sum(l8, axis=0, keepdims=True))
        ot = o[:, :Q_SUB] - lam * o[:, Q_SUB:]
        oq = ot.T
        ms = jnp.mean(oq * oq, axis=-1, keepdims=True)
        y = oq * lax.rsqrt(ms + EPS) * sg * out_scale
        gate = g_ref[q_rows, head_lanes(hh)].astype(F32)
        out_ref[q_rows, head_lanes(hh)] = (y * _silu(gate)).astype(out_ref.dtype)

    def tick(t, slot, carry, do_scores=True, do_softmax=True, do_pv=True):
        m_prev, l8_prev = carry
        m = scores(t, slot) if do_scores else None
        l8 = softmax(1 - slot, m_prev) if do_softmax else None
        if do_pv:
            pv(t - 2, slot, l8_prev)
        return m, l8

    n_items = heads * n_sub
    n_trips = (n_items - 2) // TICKS_PER_TRIP if heads == 1 and n_items - 2 >= 2 * TICKS_PER_TRIP else 0
    n_loop = n_trips * TICKS_PER_TRIP
    carry = (None, None)
    t = 0
    while t < 2:
        carry = tick(t, t % 2, carry, do_scores=t < n_items, do_softmax=1 <= t <= n_items, do_pv=False)
        t += 1
    if n_loop:
        def trip(j, c):
            t0 = 2 + TICKS_PER_TRIP * j
            for u in range(TICKS_PER_TRIP):
                c = tick(t0 + u, u % 2, c)
            return c
        carry = lax.fori_loop(0, n_trips, trip, carry)
        t += n_loop
    while t < n_items + 2:
        carry = tick(t, t % 2, carry, do_scores=t < n_items, do_softmax=1 <= t <= n_items, do_pv=True)
        t += 1


def _attention(q_arr, kv_ctx, kv_lat, lam, subln_g, *, batch, lq, heads, q_col, g_col, lc, kc_col, vc_col,
               ll, kl_col, vl_col, out_scale, name):
    width = heads * LANES
    assert ATT_HEADS % heads == 0 and all(c % heads == 0 for c in (q_col, g_col, kc_col, vc_col, kl_col, vl_col))
    col = lambda c: (lambda b, h: (b, c // heads + h))
    in_specs = [pl.BlockSpec((lq, width), col(q_col)),
                pl.BlockSpec((lc, width), col(kc_col)),
                pl.BlockSpec((lc, width), col(vc_col))]
    args = [q_arr, kv_ctx, kv_ctx]
    if kv_lat is not None:
        in_specs += [pl.BlockSpec((ll, width), col(kl_col)),
                     pl.BlockSpec((ll, width), col(vl_col))]
        args += [kv_lat, kv_lat]
    else:
        ll = 0
    in_specs += [pl.BlockSpec((lq, width), col(g_col)),
                 pl.BlockSpec((1, 8, LANES), lambda b, h: (0, 0, 0)),
                 pl.BlockSpec((1, ATT_VD), lambda b, h: (0, 0))]
    args += [q_arr, lam, subln_g]
    kern = functools.partial(_attn_kernel, lc=lc, ll=ll, n_sub=lq // Q_SUB, heads=heads, out_scale=out_scale)
    return pl.pallas_call(
        kern,
        grid=(batch, ATT_HEADS // heads),
        in_specs=in_specs,
        out_specs=pl.BlockSpec((lq, width), lambda b, h: (b, h)),
        out_shape=jax.ShapeDtypeStruct((batch * lq, ATT_HEADS * ATT_VD), BF16),
        scratch_shapes=[pltpu.VMEM((heads, ATT_VD, lc + ll), BF16),
                        pltpu.VMEM((lc + ll, 2 * Q_SUB), F32), pltpu.VMEM((lc + ll, 2 * Q_SUB), F32),
                        pltpu.VMEM((lc + ll, 2 * Q_SUB), BF16), pltpu.VMEM((lc + ll, 2 * Q_SUB), BF16)],
        compiler_params=_cparams(2),
        name=name,
    )(*args)


def _poolconv_kernel(up_c, up_p, up_n, gp_ref, a_c, a_p, a_n, b_c, b_p, b_n, gc_ref,
                     wpool_ref, pscale_ref, wdw_ref, bdw_ref, lng_ref, lnb_ref, wpw_ref,
                     ypool_ref, yconv_ref, u_ref, p_ref, z_ref, s_ref, r_ref, *, rows, tiles_per_seq, seq_len,
                     row_blk):
    iseq = pl.program_id(0) % tiles_per_seq
    has_prev = iseq > 0
    has_next = iseq < tiles_per_seq - 1

    def glu(a_ref, b_ref):
        return a_ref[...].astype(F32) * jax.nn.sigmoid(b_ref[...].astype(F32))

    u_ref[0:HALO, :] = jnp.where(has_prev, glu(a_p, b_p), 0.0)
    u_ref[HALO:HALO + rows, :] = glu(a_c, b_c)
    u_ref[HALO + rows:, :] = jnp.where(has_next, glu(a_n, b_n), 0.0)
    p_ref[0:HALO, :] = jnp.where(has_prev, up_p[...].astype(F32), 0.0)
    p_ref[HALO:HALO + rows, :] = up_c[...].astype(F32)
    p_ref[HALO + rows:, :] = jnp.where(has_next, up_n[...].astype(F32), 0.0)

    ext = rows + 2 * HALO - 8
    for b in range(1, 8):
        s_ref[b - 1] = u_ref[b:b + ext, :]

    for rb in range(rows // row_blk):
        acc = None
        for j in range(CONV_K):
            off = HALO - CONV_K // 2 + j
            base = rb * row_blk + (off // 8) * 8
            if off % 8 == 0:
                tap = u_ref[base:base + row_blk, :]
            else:
                tap = s_ref[off % 8 - 1, base:base + row_blk, :]
            term = tap * wdw_ref[j:j + 1, :]
            acc = term if acc is None else acc + term
        y = acc + bdw_ref[...]
        mu = jnp.mean(y, axis=-1, keepdims=True)
        yc = y - mu
        var = jnp.mean(yc * yc, axis=-1, keepdims=True)
        z = yc * lax.rsqrt(var + EPS) * lng_ref[...] + lnb_ref[...]
        z_ref[rb * row_blk:(rb + 1) * row_blk, :] = _silu(z).astype(BF16)
    yconv = jnp.dot(z_ref[...], wpw_ref[...], preferred_element_type=F32)
    yconv_ref[...] = (yconv * _silu(gc_ref[...].astype(F32))).astype(yconv_ref.dtype)

    t = iseq * rows + lax.broadcasted_iota(jnp.int32, (rows, POOL_GD), 0)
    for g, w in enumerate(POOL_WINDOWS):
        cols = slice(g * POOL_GD, (g + 1) * POOL_GD)
        hw = w // 2
        load = lambda s, m, _c=cols: p_ref[s:s + m, _c]
        n, level = 1, 0
        while n < hw:
            dst = r_ref.at[level % 2]
            dst[8:rows + HALO + 8, :] = load(8, rows + HALO) + load(8 + n, rows + HALO)
            dst[rows + HALO + 8:, :] = jnp.zeros((8, POOL_GD), F32)
            load = lambda s, m, _d=dst: _d[s:s + m, :]
            n, level = 2 * n, level + 1
        ssum = load(HALO - hw, rows) + load(HALO, rows)
        cnt = (jnp.minimum(t + hw, seq_len) - jnp.maximum(t - hw, 0)).astype(F32)
        dlt = ssum / cnt - p_ref[HALO:HALO + rows, cols]
        yp = jnp.dot(dlt.astype(BF16), wpool_ref[g], preferred_element_type=F32) * pscale_ref[:, cols]
        ypool_ref[:, cols] = (yp * _silu(gp_ref[:, cols].astype(F32))).astype(ypool_ref.dtype)


def _poolconv(proj, wpool, pscale, wdw, bdw, lng, lnb, wpw, *, seq_len, rows, col0, width, name):
    m_rows = proj.shape[0]
    tiles_per_seq = seq_len // rows
    hpr = rows // HALO
    n_halo = m_rows // HALO
    cb = col0 // width

    def cur(k):
        return pl.BlockSpec((rows, width), lambda i: (i, cb + k))

    def prev(k):
        return pl.BlockSpec((HALO, width), lambda i: (jnp.maximum(i * hpr - 1, 0), cb + k))

    def nxt(k):
        return pl.BlockSpec((HALO, width), lambda i: (jnp.minimum((i + 1) * hpr, n_halo - 1), cb + k))

    full = lambda a: pl.BlockSpec(a.shape, lambda i: (0,) * a.ndim)
    consts = [wpool, pscale, wdw, bdw, lng, lnb, wpw]
    kern = functools.partial(_poolconv_kernel, rows=rows, tiles_per_seq=tiles_per_seq, seq_len=seq_len,
                             row_blk=min(64, rows))
    return pl.pallas_call(
        kern,
        grid=(m_rows // rows,),
        in_specs=[cur(0), prev(0), nxt(0), cur(1), cur(2), prev(2), nxt(2), cur(3), prev(3), nxt(3), cur(4)]
                 + [full(a) for a in consts],
        out_specs=[pl.BlockSpec((rows, width), lambda i: (i, 0)),
                   pl.BlockSpec((rows, width), lambda i: (i, 0))],
        out_shape=[jax.ShapeDtypeStruct((m_rows, width), BF16),
                   jax.ShapeDtypeStruct((m_rows, width), BF16)],
        scratch_shapes=[pltpu.VMEM((rows + 2 * HALO, width), F32),
                        pltpu.VMEM((rows + 2 * HALO, width), F32),
                        pltpu.VMEM((rows, width), BF16),
                        pltpu.VMEM((7, rows + 2 * HALO - 8, width), F32),
                        pltpu.VMEM((2, rows + 2 * HALO, POOL_GD), F32)],
        compiler_params=_cparams(1),
        name=name,
    )(*([proj] * 11), *consts)


def _outproj_kernel(*refs, d_model, att_width, pool_width, final):
    if final:
        ya_ref, yp_ref, yc_ref, w_ref, x_ref, mod_ref, fg_ref, out_ref = refs
    else:
        ya_ref, yp_ref, yc_ref, w_ref, x_ref, mod_ref, out_ref = refs
    c1 = att_width
    c2 = att_width + pool_width
    y = jnp.dot(ya_ref[...], w_ref[0:c1, :].astype(BF16), preferred_element_type=F32)
    y = y + jnp.dot(yp_ref[...], w_ref[c1:c2, :].astype(BF16), preferred_element_type=F32)
    y = y + jnp.dot(yc_ref[...], w_ref[c2:, :].astype(BF16), preferred_element_type=F32)
    gate = mod_ref[0][:, 2 * d_model:]
    xn = x_ref[...] + gate * y
    if final:
        ms = jnp.mean(xn * xn, axis=-1, keepdims=True)
        xn = xn * lax.rsqrt(ms + EPS) * fg_ref[...]
    out_ref[...] = xn


def _outproj(ya, yp, yc, w, x2d, mod3, final_g, *, layer, tm, tiles_per_mod, mod_row0, name):
    m_rows, d = x2d.shape
    final = final_g is not None
    in_specs = [pl.BlockSpec((tm, ya.shape[1]), lambda i: (i, 0)),
                pl.BlockSpec((tm, yp.shape[1]), lambda i: (i, 0)),
                pl.BlockSpec((tm, yc.shape[1]), lambda i: (i, 0)),
                pl.BlockSpec((None,) + w.shape[1:], lambda i: (layer, 0, 0), pipeline_mode=pl.Buffered(1)),
                pl.BlockSpec((tm, d), lambda i: (i, 0)),
                pl.BlockSpec((1, 1, mod3.shape[-1]), lambda i: (mod_row0 + i // tiles_per_mod, 0, 0))]
    args = [ya, yp, yc, w, x2d, mod3]
    if final:
        in_specs.append(pl.BlockSpec((1, d), lambda i: (0, 0)))
        args.append(final_g)
    kern = functools.partial(_outproj_kernel, d_model=d, att_width=ya.shape[1], pool_width=yp.shape[1],
                             final=final)
    return pl.pallas_call(
        kern,
        grid=(m_rows // tm,),
        in_specs=in_specs,
        out_specs=pl.BlockSpec((tm, d), lambda i: (i, 0)),
        out_shape=jax.ShapeDtypeStruct((m_rows, d), F32),
        compiler_params=_cparams(1),
        name=name,
    )(*args)


def _rope_tables(seq_len):
    n_freq = ATT_HD // 4
    inv_freq = (np.float32(ROPE_BASE) ** (-np.arange(n_freq, dtype=np.float32) / n_freq)).astype(np.float32)
    t = np.arange(seq_len)
    lane = np.arange(LANES)
    in_col_half = (lane % ATT_HD) >= ATT_HD // 2
    pos = np.where(in_col_half[None, :], (t % GRID_W)[:, None], (t // GRID_W)[:, None]).astype(np.float32)
    ang = pos * inv_freq[lane % n_freq][None, :]
    cos = np.cos(ang).astype(np.float32)
    sin = np.sin(ang).astype(np.float32)
    qs = np.float32(ATT_HD ** -0.5 * math.log2(math.e))
    perm = np.zeros((LANES, LANES), np.float32)
    for p in range(LANES):
        if p % (2 * n_freq) < n_freq:
            perm[p + n_freq, p] = -1.0
        else:
            perm[p - n_freq, p] = 1.0
    return (jnp.asarray(np.stack([cos * qs, cos])), jnp.asarray(np.stack([sin * qs, sin])),
            jnp.asarray(perm, dtype=BF16))


def kernel(x, c, ctx, c_ctx, w_mod, b_mod, norm_g, w_in, lambda_q1, lambda_k1, lambda_q2, lambda_k2,
           subln_g, w_pool, pool_scale, w_dw, b_dw, conv_ln_g, conv_ln_b, w_pw2, w_out, final_g):
    batch, seq, d = x.shape
    lc = ctx.shape[1]
    depth = w_mod.shape[0]
    att_w = d // 2
    pool_w = d // 4
    assert batch < 8 and seq % GRID_W == 0 and att_w == ATT_HEADS * ATT_VD and pool_w == 4 * POOL_GD
    assert seq % KV_CHUNK == 0 and lc % KV_CHUNK == 0

    cc = jnp.concatenate([c, c_ctx[None, :], jnp.zeros((8 - batch - 1, d), F32)], axis=0)
    lam_init = [0.8 - 0.6 * math.exp(-0.3 * l) for l in range(depth)]
    li = jnp.broadcast_to(jnp.asarray(lam_init, F32)[:, None, None], (depth, 1, LANES))
    mod, lam = _modulation(cc, w_mod, b_mod, lambda_q1, lambda_k1, lambda_q2, lambda_k2, li)
    tabs = _rope_tables(seq)

    tm_lat = min(1024, seq)
    lat_group = math.gcd(batch * seq // tm_lat, 4)
    tn = 512
    n_in = w_in.shape[-1]
    hcol = att_w // LANES
    pc_col0 = 4 * att_w
    x2d = x.reshape(batch * seq, d)
    c2d = ctx.reshape(batch * lc, d)
    row = lambda a: a.reshape(1, -1)

    for l in range(depth):
        last = l == depth - 1
        mod3 = mod[l].reshape(8, 1, 3 * d)
        lw = (w_pool[l].astype(BF16), row(pool_scale[l]), w_dw[l], row(b_dw[l]), row(conv_ln_g[l]),
              row(conv_ln_b[l]), w_pw2[l].astype(BF16))
        g_l = row(norm_g[l])
        out_scale = 1.0 - lam_init[l]

        proj = _inproj(x2d, mod3, g_l, w_in, tabs, layer=l, tm=tm_lat, tn=tn, group=lat_group, tiles_per_mod=seq // tm_lat,
                       mod_row0=0, n0=0, n_tiles=n_in // tn, att_width=att_w, name=f"inproj_lat{l}")
        if last:
            cproj = _inproj(c2d, mod3, g_l, w_in, None, layer=l, tm=batch * lc, tn=tn, group=1, tiles_per_mod=1,
                            mod_row0=batch, n0=att_w // tn, n_tiles=2 * att_w // tn, att_width=att_w,
                            name=f"inproj_ctx{l}")
            kc_col, vc_col = 0, hcol
        else:
            cproj = _inproj(c2d, mod3, g_l, w_in, None, layer=l, tm=batch * lc, tn=tn, group=1, tiles_per_mod=1,
                            mod_row0=batch, n0=0, n_tiles=n_in // tn, att_width=att_w,
                            name=f"inproj_ctx{l}")
            kc_col, vc_col = hcol, 2 * hcol

        y_att = _attention(proj, cproj, proj, lam[l:l + 1], row(subln_g[l]), batch=batch, lq=seq,
                           heads=LAT_HEADS_PER_STEP, q_col=0, g_col=3 * hcol, lc=lc, kc_col=kc_col, vc_col=vc_col,
                           ll=seq, kl_col=hcol, vl_col=2 * hcol, out_scale=out_scale, name=f"attn_lat{l}")
        y_pool, y_conv = _poolconv(proj, *lw, seq_len=seq, rows=min(512, seq), col0=pc_col0, width=pool_w,
                                   name=f"poolconv_lat{l}")
        if not last:
            yc_att = _attention(cproj, cproj, None, lam[l:l + 1], row(subln_g[l]), batch=batch, lq=lc,
                                heads=ATT_HEADS, q_col=0, g_col=3 * hcol, lc=lc, kc_col=kc_col, vc_col=vc_col,
                                ll=0, kl_col=0, vl_col=0, out_scale=out_scale, name=f"attn_ctx{l}")
            yc_pool, yc_conv = _poolconv(cproj, *lw, seq_len=lc, rows=min(256, lc), col0=pc_col0,
                                         width=pool_w, name=f"poolconv_ctx{l}")
            c2d = _outproj(yc_att, yc_pool, yc_conv, w_out, c2d, mod3, None, layer=l, tm=min(512, batch * lc),
                           tiles_per_mod=batch * lc, mod_row0=batch, name=f"outproj_ctx{l}")
        tm_o = min(512, seq)
        x2d = _outproj(y_att, y_pool, y_conv, w_out, x2d, mod3, row(final_g) if last else None,
                       layer=l, tm=tm_o, tiles_per_mod=seq // tm_o, mod_row0=0, name=f"outproj_lat{l}")
    return x2d.reshape(batch, seq, d)
```

```python
import functools
import math

import jax
import jax.numpy as jnp
import numpy as np
from jax import lax
from jax.experimental import pallas as pl
from jax.experimental.pallas import tpu as pltpu

ATT_HEADS = 8
ATT_HD = 64
ATT_VD = 2 * ATT_HD
POOL_WINDOWS = (2, 4, 8, 16)
POOL_GD = 128
CONV_K = 31
GRID_W = 64
ROPE_BASE = 10000.0
EPS = 1e-6

LANES = 128
KV_CHUNK = 256
Q_SUB = 128
LAT_HEADS_PER_STEP = 2
SM_GROUP = 32
SM_AHEAD = 2
TICKS_PER_TRIP = 14
HALO = 16
VMEM_LIMIT = 56 * 1024 * 1024

F32 = jnp.float32
BF16 = jnp.bfloat16


def _cparams(n_axes, flags=None):
    return pltpu.CompilerParams(dimension_semantics=("arbitrary",) * n_axes,
                                vmem_limit_bytes=VMEM_LIMIT, flags=flags)


def _silu(v):
    return v * jax.nn.sigmoid(v)


def _mod_kernel(cc_ref, w_ref, b_ref, lq1_ref, lk1_ref, lq2_ref, lk2_ref, li_ref, out_ref, lam_ref):
    s = _silu(cc_ref[...])
    out_ref[0] = jnp.dot(s.astype(BF16), w_ref[0].astype(BF16), preferred_element_type=F32) + b_ref[0]
    a1 = jnp.sum(lq1_ref[0] * lk1_ref[0], axis=-1, keepdims=True)
    a2 = jnp.sum(lq2_ref[0] * lk2_ref[0], axis=-1, keepdims=True)
    lam = jnp.exp(a1) - jnp.exp(a2) + li_ref[0]
    lam_ref[0] = jnp.broadcast_to(lam, lam_ref.shape[1:])


def _modulation(cc, w_mod, b_mod, lq1, lk1, lq2, lk2, lam_init):
    depth, d, n = w_mod.shape
    tn = n // 4 if n % (4 * LANES) == 0 else n
    vec = lambda a: a.reshape(depth, 1, a.shape[-1])
    lspec = pl.BlockSpec((1, 1, ATT_HD), lambda l, j: (l, 0, 0))
    return pl.pallas_call(
        _mod_kernel,
        grid=(depth, n // tn),
        in_specs=[pl.BlockSpec((8, d), lambda l, j: (0, 0)),
                  pl.BlockSpec((1, d, tn), lambda l, j: (l, 0, j)),
                  pl.BlockSpec((1, 1, tn), lambda l, j: (l, 0, j)),
                  lspec, lspec, lspec, lspec,
                  pl.BlockSpec((1, 1, LANES), lambda l, j: (l, 0, 0))],
        out_specs=[pl.BlockSpec((1, 8, tn), lambda l, j: (l, 0, j)),
                   pl.BlockSpec((1, 8, LANES), lambda l, j: (l, 0, 0))],
        out_shape=[jax.ShapeDtypeStruct((depth, 8, n), F32),
                   jax.ShapeDtypeStruct((depth, 8, LANES), F32)],
        compiler_params=_cparams(2),
        name="modulation",
    )(cc, w_mod, vec(b_mod), vec(lq1), vec(lk1), vec(lq2), vec(lk2), lam_init)


def _inproj_kernel(*refs, d_model, rope, n_rope_tiles, n_q_tiles, row_chunk):
    if rope:
        x_ref, mod_ref, g_ref, w_ref, cos_ref, sin_ref, perm_ref, out_ref, h_ref = refs
    else:
        x_ref, mod_ref, g_ref, w_ref, out_ref, h_ref = refs
    j = pl.program_id(1)
    tm, tn = out_ref.shape
    row0 = pl.multiple_of(pl.program_id(2) * tm, tm)

    def rope_store(rows, acc):
        cos = cos_ref[rows, :]
        sin = sin_ref[rows, :]
        for hh in range(tn // LANES):
            t = acc[:, hh * LANES:(hh + 1) * LANES]
            rot = jnp.dot(t.astype(BF16), perm_ref[...], preferred_element_type=F32)
            out_ref[rows, hh * LANES:(hh + 1) * LANES] = (t * cos + rot * sin).astype(out_ref.dtype)

    def plain_store(rows, acc):
        qs = jnp.where(j < n_q_tiles, ATT_HD ** -0.5 * math.log2(math.e), 1.0).astype(F32)
        out_ref[rows, :] = (acc * qs).astype(out_ref.dtype)

    @pl.when(j == 0)
    def _():
        m = mod_ref[0]
        shift = m[:, :d_model]
        gain = g_ref[...] * (1.0 + m[:, d_model:2 * d_model])
        wb = w_ref[...].astype(BF16)
        for c in range(tm // row_chunk):
            rows = slice(c * row_chunk, (c + 1) * row_chunk)
            xf = x_ref[rows, :]
            ms = jnp.mean(xf * xf, axis=-1, keepdims=True)
            h = (xf * lax.rsqrt(ms + EPS) * gain + shift).astype(BF16)
            h_ref[pl.ds(row0 + c * row_chunk, row_chunk), :] = h
            acc = jnp.dot(h, wb, preferred_element_type=F32)
            if rope:
                rope_store(rows, acc)
            else:
                plain_store(rows, acc)

    def later_tile():
        return jnp.dot(h_ref[pl.ds(row0, tm), :], w_ref[...].astype(BF16), preferred_element_type=F32)

    if rope and n_rope_tiles > 1:
        @pl.when(jnp.logical_and(j > 0, j < n_rope_tiles))
        def _():
            rope_store(slice(None), later_tile())

    @pl.when(j >= (n_rope_tiles if rope else 1))
    def _():
        plain_store(slice(None), later_tile())


def _inproj(x2d, mod3, g, w, tabs, *, layer, tm, tn, group, tiles_per_mod, mod_row0, n0, n_tiles, att_width,
            name):
    m_rows, d = x2d.shape
    rope = tabs is not None
    n_q_tiles = max(att_width // tn - n0, 0)
    n_rope_tiles = max(2 * att_width // tn - n0, 0)
    tile = lambda gi, i: gi * group + i
    in_specs = [pl.BlockSpec((tm, d), lambda gi, j, i: (jnp.where(j == 0, tile(gi, i), tile(gi, group - 1)), 0)),
                pl.BlockSpec((1, 1, mod3.shape[-1]),
                             lambda gi, j, i: (8 * layer + mod_row0 + tile(gi, i) // tiles_per_mod, 0, 0)),
                pl.BlockSpec((None, 1, d), lambda gi, j, i: (layer, 0, 0)),
                pl.BlockSpec((None, d, tn), lambda gi, j, i: (layer, 0, n0 + j))]
    args = [x2d, mod3, g, w]
    if rope:
        tiles_per_seq = tabs[0].shape[1] // tm
        tspec = pl.BlockSpec((None, tm, LANES),
                             lambda gi, j, i: (jnp.where(j < n_q_tiles, 0, 1),
                                               jnp.where(j < n_rope_tiles, tile(gi, i) % tiles_per_seq, 0), 0))
        in_specs += [tspec, tspec, pl.BlockSpec((LANES, LANES), lambda gi, j, i: (0, 0))]
        args += list(tabs)
    kern = functools.partial(_inproj_kernel, d_model=d, rope=rope, n_rope_tiles=n_rope_tiles,
                             n_q_tiles=n_q_tiles, row_chunk=min(256, tm))
    return pl.pallas_call(
        kern,
        grid=(m_rows // (tm * group), n_tiles, group),
        in_specs=in_specs,
        out_specs=pl.BlockSpec((tm, tn), lambda gi, j, i: (tile(gi, i), j)),
        out_shape=jax.ShapeDtypeStruct((m_rows, n_tiles * tn), BF16),
        scratch_shapes=[pltpu.VMEM((group * tm, d), BF16)],
        compiler_params=_cparams(3),
        name=name,
    )(*args)


def _attn_kernel(*refs, lc, ll, n_sub, heads, out_scale):
    if ll:
        q_ref, kc_ref, vc_ref, kl_ref, vl_ref, g_ref, lam_ref, sg_ref, out_ref, vt_ref, *slots = refs
        k_parts = [(kc_ref, 0, lc), (kl_ref, lc, ll)]
        v_parts = [(vc_ref, 0, lc), (vl_ref, lc, ll)]
    else:
        q_ref, kc_ref, vc_ref, g_ref, lam_ref, sg_ref, out_ref, vt_ref, *slots = refs
        k_parts = [(kc_ref, 0, lc)]
        v_parts = [(vc_ref, 0, lc)]
    lk = lc + ll
    st_ref = slots[0:2]
    pt_ref = slots[2:4]

    def head_lanes(hh):
        return slice(hh * LANES, (hh + 1) * LANES)

    for hh in range(heads):
        for ref, off, n in v_parts:
            for c in range(n // KV_CHUNK):
                blk = ref[c * KV_CHUNK:(c + 1) * KV_CHUNK, head_lanes(hh)].astype(F32)
                vt_ref[hh, :, off + c * KV_CHUNK:off + (c + 1) * KV_CHUNK] = blk.T.astype(BF16)

    row = lax.broadcasted_iota(jnp.int32, (ATT_VD, Q_SUB), 0)
    lam = lam_ref[0][0:1, :]
    sg = sg_ref[...]

    def item(t):
        if isinstance(t, int):
            return t // n_sub, pl.ds((t % n_sub) * Q_SUB, Q_SUB)
        assert heads == 1
        return 0, pl.ds(pl.multiple_of(t * Q_SUB, Q_SUB), Q_SUB)

    def scores(t, slot):
        hh, q_rows = item(t)
        qt = q_ref[q_rows, head_lanes(hh)].astype(F32).T
        qbd = jnp.concatenate([jnp.where(row < ATT_HD, qt, 0.0),
                               jnp.where(row >= ATT_HD, qt, 0.0)], axis=1).astype(BF16)
        m8 = None
        for k_ref, off, n in k_parts:
            for c in range(n // KV_CHUNK):
                kch = k_ref[c * KV_CHUNK:(c + 1) * KV_CHUNK, head_lanes(hh)]
                st = jnp.dot(kch, qbd, preferred_element_type=F32)
                st_ref[slot][off + c * KV_CHUNK:off + (c + 1) * KV_CHUNK, :] = st
                for r in range(KV_CHUNK // 8):
                    piece = st[r * 8:(r + 1) * 8, :]
                    m8 = piece if m8 is None else jnp.maximum(m8, piece)
        return jnp.max(m8, axis=0, keepdims=True)

    def softmax(slot, m):
        l8 = None
        marks = []
        m8 = jnp.broadcast_to(m, (8, 2 * Q_SUB))
        for g in range(lk // SM_GROUP):
            rows = slice(g * SM_GROUP, (g + 1) * SM_GROUP)
            m_use = m8 if g < SM_AHEAD else m8 + marks[g - SM_AHEAD] * 0.0
            p = jnp.exp2(st_ref[slot][rows, :] - jnp.tile(m_use, (SM_GROUP // 8, 1)))
            for r in range(SM_GROUP // 8):
                piece = p[r * 8:(r + 1) * 8, :]
                l8 = piece if l8 is None else l8 + piece
            marks.append(l8)
            pt_ref[slot][rows, :] = p.astype(BF16)
        return l8

    def pv(t, slot, l8):
        hh, q_rows = item(t)
        acc = jnp.dot(vt_ref[hh], pt_ref[slot][...], preferred_element_type=F32)
        o = acc * (1.0 / jnp.sum(l8, axis=0, keepdims=True))
        ot = o[:, :Q_SUB] - lam * o[:, Q_SUB:]
        oq = ot.T
        ms = jnp.mean(oq * oq, axis=-1, keepdims=True)
        y = oq * lax.rsqrt(ms + EPS) * sg * out_scale
        gate = g_ref[q_rows, head_lanes(hh)].astype(F32)
        out_ref[q_rows, head_lanes(hh)] = (y * _silu(gate)).astype(out_ref.dtype)

    def tick(t, slot, carry, do_scores=True, do_softmax=True, do_pv=True):
        m_prev, l8_prev = carry
        m = scores(t, slot) if do_scores else None
        l8 = softmax(1 - slot, m_prev) if do_softmax else None
        if do_pv:
            pv(t - 2, slot, l8_prev)
        return m, l8

    n_items = heads * n_sub
    n_trips = (n_items - 2) // TICKS_PER_TRIP if heads == 1 and n_items - 2 >= 2 * TICKS_PER_TRIP else 0
    n_loop = n_trips * TICKS_PER_TRIP
    carry = (None, None)
    t = 0
    while t < 2:
        carry = tick(t, t % 2, carry, do_scores=t < n_items, do_softmax=1 <= t <= n_items, do_pv=False)
        t += 1
    if n_loop:
        def trip(j, c):
            t0 = 2 + TICKS_PER_TRIP * j
            for u in range(TICKS_PER_TRIP):
                c = tick(t0 + u, u % 2, c)
            return c
        carry = lax.fori_loop(0, n_trips, trip, carry)
        t += n_loop
    while t < n_items + 2:
        carry = tick(t, t % 2, carry, do_scores=t < n_items, do_softmax=1 <= t <= n_items, do_pv=True)
        t += 1


def _attention(q_arr, kv_ctx, kv_lat, lam, subln_g, *, layer, batch, lq, heads, q_col, g_col, lc, kc_col, vc_col,
               ll, kl_col, vl_col, out_scale, name):
    width = heads * LANES
    assert ATT_HEADS % heads == 0 and all(c % heads == 0 for c in (q_col, g_col, kc_col, vc_col, kl_col, vl_col))
    col = lambda c: (lambda b, h: (b, c // heads + h))
    in_specs = [pl.BlockSpec((lq, width), col(q_col)),
                pl.BlockSpec((lc, width), col(kc_col)),
                pl.BlockSpec((lc, width), col(vc_col))]
    args = [q_arr, kv_ctx, kv_ctx]
    if kv_lat is not None:
        in_specs += [pl.BlockSpec((ll, width), col(kl_col)),
                     pl.BlockSpec((ll, width), col(vl_col))]
        args += [kv_lat, kv_lat]
    else:
        ll = 0
    in_specs += [pl.BlockSpec((lq, width), col(g_col)),
                 pl.BlockSpec((1, 8, LANES), lambda b, h: (layer, 0, 0)),
                 pl.BlockSpec((None, 1, ATT_VD), lambda b, h: (layer, 0, 0))]
    args += [q_arr, lam, subln_g]
    kern = functools.partial(_attn_kernel, lc=lc, ll=ll, n_sub=lq // Q_SUB, heads=heads, out_scale=out_scale)
    return pl.pallas_call(
        kern,
        grid=(batch, ATT_HEADS // heads),
        in_specs=in_specs,
        out_specs=pl.BlockSpec((lq, width), lambda b, h: (b, h)),
        out_shape=jax.ShapeDtypeStruct((batch * lq, ATT_HEADS * ATT_VD), BF16),
        scratch_shapes=[pltpu.VMEM((heads, ATT_VD, lc + ll), BF16),
                        pltpu.VMEM((lc + ll, 2 * Q_SUB), F32), pltpu.VMEM((lc + ll, 2 * Q_SUB), F32),
                        pltpu.VMEM((lc + ll, 2 * Q_SUB), BF16), pltpu.VMEM((lc + ll, 2 * Q_SUB), BF16)],
        compiler_params=_cparams(2),
        name=name,
    )(*args)


def _poolconv_kernel(up_c, up_p, up_n, gp_ref, a_c, a_p, a_n, b_c, b_p, b_n, gc_ref,
                     wpool_ref, pscale_ref, wdw_ref, bdw_ref, lng_ref, lnb_ref, wpw_ref,
                     ypool_ref, yconv_ref, u_ref, p_ref, z_ref, s_ref, r_ref, *, rows, tiles_per_seq, seq_len,
                     row_blk):
    iseq = pl.program_id(0) % tiles_per_seq
    has_prev = iseq > 0
    has_next = iseq < tiles_per_seq - 1

    def glu(a_ref, b_ref):
        return a_ref[...].astype(F32) * jax.nn.sigmoid(b_ref[...].astype(F32))

    u_ref[0:HALO, :] = jnp.where(has_prev, glu(a_p, b_p), 0.0)
    u_ref[HALO:HALO + rows, :] = glu(a_c, b_c)
    u_ref[HALO + rows:, :] = jnp.where(has_next, glu(a_n, b_n), 0.0)
    p_ref[0:HALO, :] = jnp.where(has_prev, up_p[...].astype(F32), 0.0)
    p_ref[HALO:HALO + rows, :] = up_c[...].astype(F32)
    p_ref[HALO + rows:, :] = jnp.where(has_next, up_n[...].astype(F32), 0.0)

    ext = rows + 2 * HALO - 8
    for b in range(1, 8):
        s_ref[b - 1] = u_ref[b:b + ext, :]

    for rb in range(rows // row_blk):
        acc = None
        for j in range(CONV_K):
            off = HALO - CONV_K // 2 + j
            base = rb * row_blk + (off // 8) * 8
            if off % 8 == 0:
                tap = u_ref[base:base + row_blk, :]
            else:
                tap = s_ref[off % 8 - 1, base:base + row_blk, :]
            term = tap * wdw_ref[j:j + 1, :]
            acc = term if acc is None else acc + term
        y = acc + bdw_ref[...]
        mu = jnp.mean(y, axis=-1, keepdims=True)
        yc = y - mu
        var = jnp.mean(yc * yc, axis=-1, keepdims=True)
        z = yc * lax.rsqrt(var + EPS) * lng_ref[...] + lnb_ref[...]
        z_ref[rb * row_blk:(rb + 1) * row_blk, :] = _silu(z).astype(BF16)
    yconv = jnp.dot(z_ref[...], wpw_ref[...].astype(BF16), preferred_element_type=F32)
    yconv_ref[...] = (yconv * _silu(gc_ref[...].astype(F32))).astype(yconv_ref.dtype)

    t = iseq * rows + lax.broadcasted_iota(jnp.int32, (rows, POOL_GD), 0)
    for g, w in enumerate(POOL_WINDOWS):
        cols = slice(g * POOL_GD, (g + 1) * POOL_GD)
        hw = w // 2
        load = lambda s, m, _c=cols: p_ref[s:s + m, _c]
        n, level = 1, 0
        while n < hw:
            dst = r_ref.at[level % 2]
            dst[8:rows + HALO + 8, :] = load(8, rows + HALO) + load(8 + n, rows + HALO)
            dst[rows + HALO + 8:, :] = jnp.zeros((8, POOL_GD), F32)
            load = lambda s, m, _d=dst: _d[s:s + m, :]
            n, level = 2 * n, level + 1
        ssum = load(HALO - hw, rows) + load(HALO, rows)
        cnt = (jnp.minimum(t + hw, seq_len) - jnp.maximum(t - hw, 0)).astype(F32)
        dlt = ssum / cnt - p_ref[HALO:HALO + rows, cols]
        yp = jnp.dot(dlt.astype(BF16), wpool_ref[g].astype(BF16), preferred_element_type=F32) * pscale_ref[:, cols]
        ypool_ref[:, cols] = (yp * _silu(gp_ref[:, cols].astype(F32))).astype(ypool_ref.dtype)


def _poolconv(proj, wpool, pscale, wdw, bdw, lng, lnb, wpw, *, layer, seq_len, rows, col0, width, name):
    m_rows = proj.shape[0]
    tiles_per_seq = seq_len // rows
    hpr = rows // HALO
    n_halo = m_rows // HALO
    cb = col0 // width

    def cur(k):
        return pl.BlockSpec((rows, width), lambda i: (i, cb + k))

    def prev(k):
        return pl.BlockSpec((HALO, width), lambda i: (jnp.maximum(i * hpr - 1, 0), cb + k))

    def nxt(k):
        return pl.BlockSpec((HALO, width), lambda i: (jnp.minimum((i + 1) * hpr, n_halo - 1), cb + k))

    full = lambda a: pl.BlockSpec((None,) + a.shape[1:], lambda i: (layer,) + (0,) * (a.ndim - 1))
    consts = [wpool, pscale, wdw, bdw, lng, lnb, wpw]
    kern = functools.partial(_poolconv_kernel, rows=rows, tiles_per_seq=tiles_per_seq, seq_len=seq_len,
                             row_blk=min(64, rows))
    return pl.pallas_call(
        kern,
        grid=(m_rows // rows,),
        in_specs=[cur(0), prev(0), nxt(0), cur(1), cur(2), prev(2), nxt(2), cur(3), prev(3), nxt(3), cur(4)]
                 + [full(a) for a in consts],
        out_specs=[pl.BlockSpec((rows, width), lambda i: (i, 0)),
                   pl.BlockSpec((rows, width), lambda i: (i, 0))],
        out_shape=[jax.ShapeDtypeStruct((m_rows, width), BF16),
                   jax.ShapeDtypeStruct((m_rows, width), BF16)],
        scratch_shapes=[pltpu.VMEM((rows + 2 * HALO, width), F32),
                        pltpu.VMEM((rows + 2 * HALO, width), F32),
                        pltpu.VMEM((rows, width), BF16),
                        pltpu.VMEM((7, rows + 2 * HALO - 8, width), F32),
                        pltpu.VMEM((2, rows + 2 * HALO, POOL_GD), F32)],
        compiler_params=_cparams(1),
        name=name,
    )(*([proj] * 11), *consts)


def _outproj_kernel(*refs, d_model, att_width, pool_width, final):
    if final:
        ya_ref, yp_ref, yc_ref, w_ref, x_ref, mod_ref, fg_ref, out_ref = refs
    else:
        ya_ref, yp_ref, yc_ref, w_ref, x_ref, mod_ref, out_ref = refs
    c1 = att_width
    c2 = att_width + pool_width
    y = jnp.dot(ya_ref[...], w_ref[0:c1, :].astype(BF16), preferred_element_type=F32)
    y = y + jnp.dot(yp_ref[...], w_ref[c1:c2, :].astype(BF16), preferred_element_type=F32)
    y = y + jnp.dot(yc_ref[...], w_ref[c2:, :].astype(BF16), preferred_element_type=F32)
    gate = mod_ref[0][:, 2 * d_model:]
    xn = x_ref[...] + gate * y
    if final:
        ms = jnp.mean(xn * xn, axis=-1, keepdims=True)
        xn = xn * lax.rsqrt(ms + EPS) * fg_ref[...]
    out_ref[...] = xn


def _outproj(ya, yp, yc, w, x2d, mod3, final_g, *, layer, tm, tiles_per_mod, mod_row0, name):
    m_rows, d = x2d.shape
    final = final_g is not None
    in_specs = [pl.BlockSpec((tm, ya.shape[1]), lambda i: (i, 0)),
                pl.BlockSpec((tm, yp.shape[1]), lambda i: (i, 0)),
                pl.BlockSpec((tm, yc.shape[1]), lambda i: (i, 0)),
                pl.BlockSpec((None,) + w.shape[1:], lambda i: (layer, 0, 0), pipeline_mode=pl.Buffered(1)),
                pl.BlockSpec((tm, d), lambda i: (i, 0)),
                pl.BlockSpec((1, 1, mod3.shape[-1]), lambda i: (8 * layer + mod_row0 + i // tiles_per_mod, 0, 0))]
    args = [ya, yp, yc, w, x2d, mod3]
    if final:
        in_specs.append(pl.BlockSpec((1, d), lambda i: (0, 0)))
        args.append(final_g)
    kern = functools.partial(_outproj_kernel, d_model=d, att_width=ya.shape[1], pool_width=yp.shape[1],
                             final=final)
    return pl.pallas_call(
        kern,
        grid=(m_rows // tm,),
        in_specs=in_specs,
        out_specs=pl.BlockSpec((tm, d), lambda i: (i, 0)),
        out_shape=jax.ShapeDtypeStruct((m_rows, d), F32),
        compiler_params=_cparams(1),
        name=name,
    )(*args)


def _rope_tables(seq_len):
    n_freq = ATT_HD // 4
    inv_freq = (np.float32(ROPE_BASE) ** (-np.arange(n_freq, dtype=np.float32) / n_freq)).astype(np.float32)
    t = np.arange(seq_len)
    lane = np.arange(LANES)
    in_col_half = (lane % ATT_HD) >= ATT_HD // 2
    pos = np.where(in_col_half[None, :], (t % GRID_W)[:, None], (t // GRID_W)[:, None]).astype(np.float32)
    ang = pos * inv_freq[lane % n_freq][None, :]
    cos = np.cos(ang).astype(np.float32)
    sin = np.sin(ang).astype(np.float32)
    qs = np.float32(ATT_HD ** -0.5 * math.log2(math.e))
    perm = np.zeros((LANES, LANES), np.float32)
    for p in range(LANES):
        if p % (2 * n_freq) < n_freq:
            perm[p + n_freq, p] = -1.0
        else:
            perm[p - n_freq, p] = 1.0
    return (jnp.asarray(np.stack([cos * qs, cos])), jnp.asarray(np.stack([sin * qs, sin])),
            jnp.asarray(perm, dtype=BF16))


def kernel(x, c, ctx, c_ctx, w_mod, b_mod, norm_g, w_in, lambda_q1, lambda_k1, lambda_q2, lambda_k2,
           subln_g, w_pool, pool_scale, w_dw, b_dw, conv_ln_g, conv_ln_b, w_pw2, w_out, final_g):
    batch, seq, d = x.shape
    lc = ctx.shape[1]
    depth = w_mod.shape[0]
    att_w = d // 2
    pool_w = d // 4
    assert batch < 8 and seq % GRID_W == 0 and att_w == ATT_HEADS * ATT_VD and pool_w == 4 * POOL_GD
    assert seq % KV_CHUNK == 0 and lc % KV_CHUNK == 0

    cc = jnp.concatenate([c, c_ctx[None, :], jnp.zeros((8 - batch - 1, d), F32)], axis=0)
    lam_init = [0.8 - 0.6 * math.exp(-0.3 * l) for l in range(depth)]
    li = jnp.broadcast_to(jnp.asarray(lam_init, F32)[:, None, None], (depth, 1, LANES))
    mod, lam = _modulation(cc, w_mod, b_mod, lambda_q1, lambda_k1, lambda_q2, lambda_k2, li)
    tabs = _rope_tables(seq)

    tm_lat = min(1024, seq)
    lat_group = math.gcd(batch * seq // tm_lat, 4)
    tn = 512
    n_in = w_in.shape[-1]
    hcol = att_w // LANES
    pc_col0 = 4 * att_w
    x2d = x.reshape(batch * seq, d)
    c2d = ctx.reshape(batch * lc, d)
    row = lambda a: a.reshape(1, -1)
    vec = lambda a: a.reshape(depth, 1, a.shape[-1])
    mod3 = mod.reshape(depth * 8, 1, 3 * d)
    lw = (w_pool, vec(pool_scale), w_dw, vec(b_dw), vec(conv_ln_g), vec(conv_ln_b), w_pw2)
    g_l = vec(norm_g)
    sg_l = vec(subln_g)

    for l in range(depth):
        last = l == depth - 1
        out_scale = 1.0 - lam_init[l]

        proj = _inproj(x2d, mod3, g_l, w_in, tabs, layer=l, tm=tm_lat, tn=tn, group=lat_group, tiles_per_mod=seq // tm_lat,
                       mod_row0=0, n0=0, n_tiles=n_in // tn, att_width=att_w, name=f"inproj_lat{l}")
        if last:
            cproj = _inproj(c2d, mod3, g_l, w_in, None, layer=l, tm=batch * lc, tn=tn, group=1, tiles_per_mod=1,
                            mod_row0=batch, n0=att_w // tn, n_tiles=2 * att_w // tn, att_width=att_w,
                            name=f"inproj_ctx{l}")
            kc_col, vc_col = 0, hcol
        else:
            cproj = _inproj(c2d, mod3, g_l, w_in, None, layer=l, tm=batch * lc, tn=tn, group=1, tiles_per_mod=1,
                            mod_row0=batch, n0=0, n_tiles=n_in // tn, att_width=att_w,
                            name=f"inproj_ctx{l}")
            kc_col, vc_col = hcol, 2 * hcol

        y_att = _attention(proj, cproj, proj, lam, sg_l, layer=l, batch=batch, lq=seq,
                           heads=LAT_HEADS_PER_STEP, q_col=0, g_col=3 * hcol, lc=lc, kc_col=kc_col, vc_col=vc_col,
                           ll=seq, kl_col=hcol, vl_col=2 * hcol, out_scale=out_scale, name=f"attn_lat{l}")
        y_pool, y_conv = _poolconv(proj, *lw, layer=l, seq_len=seq, rows=min(512, seq), col0=pc_col0, width=pool_w,
                                   name=f"poolconv_lat{l}")
        if not last:
            yc_att = _attention(cproj, cproj, None, lam, sg_l, layer=l, batch=batch, lq=lc,
                                heads=ATT_HEADS, q_col=0, g_col=3 * hcol, lc=lc, kc_col=kc_col, vc_col=vc_col,
                                ll=0, kl_col=0, vl_col=0, out_scale=out_scale, name=f"attn_ctx{l}")
            yc_pool, yc_conv = _poolconv(cproj, *lw, layer=l, seq_len=lc, rows=min(256, lc), col0=pc_col0,
                                         width=pool_w, name=f"poolconv_ctx{l}")
            c2d = _outproj(yc_att, yc_pool, yc_conv, w_out, c2d, mod3, None, layer=l, tm=min(512, batch * lc),
                           tiles_per_mod=batch * lc, mod_row0=batch, name=f"outproj_ctx{l}")
        tm_o = min(512, seq)
        x2d = _outproj(y_att, y_pool, y_conv, w_out, x2d, mod3, row(final_g) if last else None,
                       layer=l, tm=tm_o, tiles_per_mod=seq // tm_o, mod_row0=0, name=f"outproj_lat{l}")
    return x2d.reshape(batch, seq, d)
```

```python
import functools
import math

import jax
import jax.numpy as jnp
import numpy as np
from jax import lax
from jax.experimental import pallas as pl
from jax.experimental.pallas import tpu as pltpu

ATT_HEADS = 8
ATT_HD = 64
ATT_VD = 2 * ATT_HD
POOL_WINDOWS = (2, 4, 8, 16)
POOL_GD = 128
CONV_K = 31
GRID_W = 64
ROPE_BASE = 10000.0
EPS = 1e-6

LANES = 128
KV_CHUNK = 256
Q_SUB = 128
LAT_HEADS_PER_STEP = 2
SM_GROUP = 32
SM_AHEAD = 2
TICKS_PER_TRIP = 14
HALO = 16
VMEM_LIMIT = 56 * 1024 * 1024

F32 = jnp.float32
BF16 = jnp.bfloat16


def _cparams(n_axes, flags=None):
    return pltpu.CompilerParams(dimension_semantics=("arbitrary",) * n_axes,
                                vmem_limit_bytes=VMEM_LIMIT, flags=flags)


def _silu(v):
    return v * jax.nn.sigmoid(v)


def _mod_kernel(cc_ref, w_ref, b_ref, lq1_ref, lk1_ref, lq2_ref, lk2_ref, li_ref, out_ref, lam_ref):
    s = _silu(cc_ref[...])
    out_ref[0] = jnp.dot(s.astype(BF16), w_ref[0].astype(BF16), preferred_element_type=F32) + b_ref[0]
    a1 = jnp.sum(lq1_ref[0] * lk1_ref[0], axis=-1, keepdims=True)
    a2 = jnp.sum(lq2_ref[0] * lk2_ref[0], axis=-1, keepdims=True)
    lam = jnp.exp(a1) - jnp.exp(a2) + li_ref[0]
    lam_ref[0] = jnp.broadcast_to(lam, lam_ref.shape[1:])


def _modulation(cc, w_mod, b_mod, lq1, lk1, lq2, lk2, lam_init):
    depth, d, n = w_mod.shape
    tn = n // 4 if n % (4 * LANES) == 0 else n
    vec = lambda a: a.reshape(depth, 1, a.shape[-1])
    lspec = pl.BlockSpec((1, 1, ATT_HD), lambda l, j: (l, 0, 0))
    return pl.pallas_call(
        _mod_kernel,
        grid=(depth, n // tn),
        in_specs=[pl.BlockSpec((8, d), lambda l, j: (0, 0)),
                  pl.BlockSpec((1, d, tn), lambda l, j: (l, 0, j)),
                  pl.BlockSpec((1, 1, tn), lambda l, j: (l, 0, j)),
                  lspec, lspec, lspec, lspec,
                  pl.BlockSpec((1, 1, LANES), lambda l, j: (l, 0, 0))],
        out_specs=[pl.BlockSpec((1, 8, tn), lambda l, j: (l, 0, j)),
                   pl.BlockSpec((1, 8, LANES), lambda l, j: (l, 0, 0))],
        out_shape=[jax.ShapeDtypeStruct((depth, 8, n), F32),
                   jax.ShapeDtypeStruct((depth, 8, LANES), F32)],
        compiler_params=_cparams(2),
        name="modulation",
    )(cc, w_mod, vec(b_mod), vec(lq1), vec(lk1), vec(lq2), vec(lk2), lam_init)


def _inproj_kernel(*refs, layer, d_model, rope, n_rope_tiles, n_q_tiles, row_chunk):
    if rope:
        x_ref, mod_ref, g_ref, w_ref, cos_ref, sin_ref, perm_ref, out_ref, h_ref = refs
    else:
        x_ref, mod_ref, g_ref, w_ref, out_ref, h_ref = refs
    j = pl.program_id(1)
    tm, tn = out_ref.shape
    row0 = pl.multiple_of(pl.program_id(2) * tm, tm)

    def rope_store(rows, acc):
        cos = cos_ref[rows, :]
        sin = sin_ref[rows, :]
        for hh in range(tn // LANES):
            t = acc[:, hh * LANES:(hh + 1) * LANES]
            rot = jnp.dot(t.astype(BF16), perm_ref[...], preferred_element_type=F32)
            out_ref[rows, hh * LANES:(hh + 1) * LANES] = (t * cos + rot * sin).astype(out_ref.dtype)

    def plain_store(rows, acc):
        qs = jnp.where(j < n_q_tiles, ATT_HD ** -0.5 * math.log2(math.e), 1.0).astype(F32)
        out_ref[rows, :] = (acc * qs).astype(out_ref.dtype)

    @pl.when(j == 0)
    def _():
        m = mod_ref[0]
        shift = m[:, :d_model]
        gain = g_ref[layer:layer + 1, :] * (1.0 + m[:, d_model:2 * d_model])
        wb = w_ref[...].astype(BF16)
        for c in range(tm // row_chunk):
            rows = slice(c * row_chunk, (c + 1) * row_chunk)
            xf = x_ref[rows, :]
            ms = jnp.mean(xf * xf, axis=-1, keepdims=True)
            h = (xf * lax.rsqrt(ms + EPS) * gain + shift).astype(BF16)
            h_ref[pl.ds(row0 + c * row_chunk, row_chunk), :] = h
            acc = jnp.dot(h, wb, preferred_element_type=F32)
            if rope:
                rope_store(rows, acc)
            else:
                plain_store(rows, acc)

    def later_tile():
        return jnp.dot(h_ref[pl.ds(row0, tm), :], w_ref[...].astype(BF16), preferred_element_type=F32)

    if rope and n_rope_tiles > 1:
        @pl.when(jnp.logical_and(j > 0, j < n_rope_tiles))
        def _():
            rope_store(slice(None), later_tile())

    @pl.when(j >= (n_rope_tiles if rope else 1))
    def _():
        plain_store(slice(None), later_tile())


def _inproj(x2d, mod3, g, w, tabs, *, layer, tm, tn, group, tiles_per_mod, mod_row0, n0, n_tiles, att_width,
            name):
    m_rows, d = x2d.shape
    rope = tabs is not None
    n_q_tiles = max(att_width // tn - n0, 0)
    n_rope_tiles = max(2 * att_width // tn - n0, 0)
    tile = lambda gi, i: gi * group + i
    in_specs = [pl.BlockSpec((tm, d), lambda gi, j, i: (jnp.where(j == 0, tile(gi, i), tile(gi, group - 1)), 0)),
                pl.BlockSpec((1, 1, mod3.shape[-1]),
                             lambda gi, j, i: (8 * layer + mod_row0 + tile(gi, i) // tiles_per_mod, 0, 0)),
                pl.BlockSpec(g.shape, lambda gi, j, i: (0, 0)),
                pl.BlockSpec((None, d, tn), lambda gi, j, i: (layer, 0, n0 + j))]
    args = [x2d, mod3, g, w]
    if rope:
        tiles_per_seq = tabs[0].shape[1] // tm
        tspec = pl.BlockSpec((None, tm, LANES),
                             lambda gi, j, i: (jnp.where(j < n_q_tiles, 0, 1),
                                               jnp.where(j < n_rope_tiles, tile(gi, i) % tiles_per_seq, 0), 0))
        in_specs += [tspec, tspec, pl.BlockSpec((LANES, LANES), lambda gi, j, i: (0, 0))]
        args += list(tabs)
    kern = functools.partial(_inproj_kernel, layer=layer, d_model=d, rope=rope, n_rope_tiles=n_rope_tiles,
                             n_q_tiles=n_q_tiles, row_chunk=min(256, tm))
    return pl.pallas_call(
        kern,
        grid=(m_rows // (tm * group), n_tiles, group),
        in_specs=in_specs,
        out_specs=pl.BlockSpec((tm, tn), lambda gi, j, i: (tile(gi, i), j)),
        out_shape=jax.ShapeDtypeStruct((m_rows, n_tiles * tn), BF16),
        scratch_shapes=[pltpu.VMEM((group * tm, d), BF16)],
        compiler_params=_cparams(3),
        name=name,
    )(*args)


def _attn_kernel(*refs, layer, lc, ll, n_sub, heads, out_scale):
    if ll:
        q_ref, kc_ref, vc_ref, kl_ref, vl_ref, g_ref, lam_ref, sg_ref, out_ref, vt_ref, *slots = refs
        k_parts = [(kc_ref, 0, lc), (kl_ref, lc, ll)]
        v_parts = [(vc_ref, 0, lc), (vl_ref, lc, ll)]
    else:
        q_ref, kc_ref, vc_ref, g_ref, lam_ref, sg_ref, out_ref, vt_ref, *slots = refs
        k_parts = [(kc_ref, 0, lc)]
        v_parts = [(vc_ref, 0, lc)]
    lk = lc + ll
    st_ref = slots[0:2]
    pt_ref = slots[2:4]

    def head_lanes(hh):
        return slice(hh * LANES, (hh + 1) * LANES)

    for hh in range(heads):
        for ref, off, n in v_parts:
            for c in range(n // KV_CHUNK):
                blk = ref[c * KV_CHUNK:(c + 1) * KV_CHUNK, head_lanes(hh)].astype(F32)
                vt_ref[hh, :, off + c * KV_CHUNK:off + (c + 1) * KV_CHUNK] = blk.T.astype(BF16)

    row = lax.broadcasted_iota(jnp.int32, (ATT_VD, Q_SUB), 0)
    lam = lam_ref[0][0:1, :]
    sg = sg_ref[layer:layer + 1, :]

    def item(t):
        if isinstance(t, int):
            return t // n_sub, pl.ds((t % n_sub) * Q_SUB, Q_SUB)
        assert heads == 1
        return 0, pl.ds(pl.multiple_of(t * Q_SUB, Q_SUB), Q_SUB)

    def scores(t, slot):
        hh, q_rows = item(t)
        qt = q_ref[q_rows, head_lanes(hh)].astype(F32).T
        qbd = jnp.concatenate([jnp.where(row < ATT_HD, qt, 0.0),
                               jnp.where(row >= ATT_HD, qt, 0.0)], axis=1).astype(BF16)
        m8 = None
        for k_ref, off, n in k_parts:
            for c in range(n // KV_CHUNK):
                kch = k_ref[c * KV_CHUNK:(c + 1) * KV_CHUNK, head_lanes(hh)]
                st = jnp.dot(kch, qbd, preferred_element_type=F32)
                st_ref[slot][off + c * KV_CHUNK:off + (c + 1) * KV_CHUNK, :] = st
                for r in range(KV_CHUNK // 8):
                    piece = st[r * 8:(r + 1) * 8, :]
                    m8 = piece if m8 is None else jnp.maximum(m8, piece)
        return jnp.max(m8, axis=0, keepdims=True)

    def softmax(slot, m):
        l8 = None
        marks = []
        m8 = jnp.broadcast_to(m, (8, 2 * Q_SUB))
        for g in range(lk // SM_GROUP):
            rows = slice(g * SM_GROUP, (g + 1) * SM_GROUP)
            m_use = m8 if g < SM_AHEAD else m8 + marks[g - SM_AHEAD] * 0.0
            p = jnp.exp2(st_ref[slot][rows, :] - jnp.tile(m_use, (SM_GROUP // 8, 1)))
            for r in range(SM_GROUP // 8):
                piece = p[r * 8:(r + 1) * 8, :]
                l8 = piece if l8 is None else l8 + piece
            marks.append(l8)
            pt_ref[slot][rows, :] = p.astype(BF16)
        return l8

    def pv(t, slot, l8):
        hh, q_rows = item(t)
        acc = jnp.dot(vt_ref[hh], pt_ref[slot][...], preferred_element_type=F32)
        o = acc * (1.0 / jnp.sum(l8, axis=0, keepdims=True))
        ot = o[:, :Q_SUB] - lam * o[:, Q_SUB:]
        oq = ot.T
        ms = jnp.mean(oq * oq, axis=-1, keepdims=True)
        y = oq * lax.rsqrt(ms + EPS) * sg * out_scale
        gate = g_ref[q_rows, head_lanes(hh)].astype(F32)
        out_ref[q_rows, head_lanes(hh)] = (y * _silu(gate)).astype(out_ref.dtype)

    def tick(t, slot, carry, do_scores=True, do_softmax=True, do_pv=True):
        m_prev, l8_prev = carry
        m = scores(t, slot) if do_scores else None
        l8 = softmax(1 - slot, m_prev) if do_softmax else None
        if do_pv:
            pv(t - 2, slot, l8_prev)
        return m, l8

    n_items = heads * n_sub
    n_trips = (n_items - 2) // TICKS_PER_TRIP if heads == 1 and n_items - 2 >= 2 * TICKS_PER_TRIP else 0
    n_loop = n_trips * TICKS_PER_TRIP
    carry = (None, None)
    t = 0
    while t < 2:
        carry = tick(t, t % 2, carry, do_scores=t < n_items, do_softmax=1 <= t <= n_items, do_pv=False)
        t += 1
    if n_loop:
        def trip(j, c):
            t0 = 2 + TICKS_PER_TRIP * j
            for u in range(TICKS_PER_TRIP):
                c = tick(t0 + u, u % 2, c)
            return c
        carry = lax.fori_loop(0, n_trips, trip, carry)
        t += n_loop
    while t < n_items + 2:
        carry = tick(t, t % 2, carry, do_scores=t < n_items, do_softmax=1 <= t <= n_items, do_pv=True)
        t += 1


def _attention(q_arr, kv_ctx, kv_lat, lam, subln_g, *, layer, batch, lq, heads, q_col, g_col, lc, kc_col, vc_col,
               ll, kl_col, vl_col, out_scale, name):
    width = heads * LANES
    assert ATT_HEADS % heads == 0 and all(c % heads == 0 for c in (q_col, g_col, kc_col, vc_col, kl_col, vl_col))
    col = lambda c: (lambda b, h: (b, c // heads + h))
    in_specs = [pl.BlockSpec((lq, width), col(q_col)),
                pl.BlockSpec((lc, width), col(kc_col)),
                pl.BlockSpec((lc, width), col(vc_col))]
    args = [q_arr, kv_ctx, kv_ctx]
    if kv_lat is not None:
        in_specs += [pl.BlockSpec((ll, width), col(kl_col)),
                     pl.BlockSpec((ll, width), col(vl_col))]
        args += [kv_lat, kv_lat]
    else:
        ll = 0
    in_specs += [pl.BlockSpec((lq, width), col(g_col)),
                 pl.BlockSpec((1, 8, LANES), lambda b, h: (layer, 0, 0)),
                 pl.BlockSpec(subln_g.shape, lambda b, h: (0, 0))]
    args += [q_arr, lam, subln_g]
    kern = functools.partial(_attn_kernel, layer=layer, lc=lc, ll=ll, n_sub=lq // Q_SUB, heads=heads, out_scale=out_scale)
    return pl.pallas_call(
        kern,
        grid=(batch, ATT_HEADS // heads),
        in_specs=in_specs,
        out_specs=pl.BlockSpec((lq, width), lambda b, h: (b, h)),
        out_shape=jax.ShapeDtypeStruct((batch * lq, ATT_HEADS * ATT_VD), BF16),
        scratch_shapes=[pltpu.VMEM((heads, ATT_VD, lc + ll), BF16),
                        pltpu.VMEM((lc + ll, 2 * Q_SUB), F32), pltpu.VMEM((lc + ll, 2 * Q_SUB), F32),
                        pltpu.VMEM((lc + ll, 2 * Q_SUB), BF16), pltpu.VMEM((lc + ll, 2 * Q_SUB), BF16)],
        compiler_params=_cparams(2),
        name=name,
    )(*args)


def _poolconv_kernel(up_c, up_p, up_n, gp_ref, a_c, a_p, a_n, b_c, b_p, b_n, gc_ref,
                     wpool_ref, pscale_ref, wdw_ref, bdw_ref, lng_ref, lnb_ref, wpw_ref,
                     ypool_ref, yconv_ref, u_ref, p_ref, z_ref, s_ref, r_ref, w8_ref, *, layer, rows, tiles_per_seq,
                     seq_len, row_blk):
    iseq = pl.program_id(0) % tiles_per_seq
    has_prev = iseq > 0
    has_next = iseq < tiles_per_seq - 1

    def glu(a_ref, b_ref):
        return a_ref[...].astype(F32) * jax.nn.sigmoid(b_ref[...].astype(F32))

    u_ref[0:HALO, :] = jnp.where(has_prev, glu(a_p, b_p), 0.0)
    u_ref[HALO:HALO + rows, :] = glu(a_c, b_c)
    u_ref[HALO + rows:, :] = jnp.where(has_next, glu(a_n, b_n), 0.0)
    p_ref[0:HALO, :] = jnp.where(has_prev, up_p[...].astype(F32), 0.0)
    p_ref[HALO:HALO + rows, :] = up_c[...].astype(F32)
    p_ref[HALO + rows:, :] = jnp.where(has_next, up_n[...].astype(F32), 0.0)

    ext = rows + 2 * HALO - 8
    for b in range(1, 8):
        s_ref[b - 1] = u_ref[b:b + ext, :]

    width = u_ref.shape[1]

    @pl.when(pl.program_id(0) == 0)
    def _():
        for j in range(CONV_K):
            w8_ref[8 * j:8 * j + 8, :] = jnp.broadcast_to(wdw_ref[j:j + 1, :], (8, width))

    lrow = slice(layer, layer + 1)
    for rb in range(rows // row_blk):
        acc = None
        for j in range(CONV_K):
            off = HALO - CONV_K // 2 + j
            base = rb * row_blk + (off // 8) * 8
            if off % 8 == 0:
                tap = u_ref[base:base + row_blk, :]
            else:
                tap = s_ref[off % 8 - 1, base:base + row_blk, :]
            term = (tap.reshape(row_blk // 8, 8, width) * w8_ref[8 * j:8 * j + 8, :]).reshape(row_blk, width)
            acc = term if acc is None else acc + term
        y = acc + bdw_ref[lrow, :]
        mu = jnp.mean(y, axis=-1, keepdims=True)
        yc = y - mu
        var = jnp.mean(yc * yc, axis=-1, keepdims=True)
        z = yc * lax.rsqrt(var + EPS) * lng_ref[lrow, :] + lnb_ref[lrow, :]
        z_ref[rb * row_blk:(rb + 1) * row_blk, :] = _silu(z).astype(BF16)
    yconv = jnp.dot(z_ref[...], wpw_ref[...].astype(BF16), preferred_element_type=F32)
    yconv_ref[...] = (yconv * _silu(gc_ref[...].astype(F32))).astype(yconv_ref.dtype)

    t = iseq * rows + lax.broadcasted_iota(jnp.int32, (rows, POOL_GD), 0)
    for g, w in enumerate(POOL_WINDOWS):
        cols = slice(g * POOL_GD, (g + 1) * POOL_GD)
        hw = w // 2
        load = lambda s, m, _c=cols: p_ref[s:s + m, _c]
        n, level = 1, 0
        while n < hw:
            dst = r_ref.at[level % 2]
            dst[8:rows + HALO + 8, :] = load(8, rows + HALO) + load(8 + n, rows + HALO)
            dst[rows + HALO + 8:, :] = jnp.zeros((8, POOL_GD), F32)
            load = lambda s, m, _d=dst: _d[s:s + m, :]
            n, level = 2 * n, level + 1
        ssum = load(HALO - hw, rows) + load(HALO, rows)
        cnt = (jnp.minimum(t + hw, seq_len) - jnp.maximum(t - hw, 0)).astype(F32)
        dlt = ssum / cnt - p_ref[HALO:HALO + rows, cols]
        yp = jnp.dot(dlt.astype(BF16), wpool_ref[g].astype(BF16), preferred_element_type=F32) * pscale_ref[lrow, cols]
        ypool_ref[:, cols] = (yp * _silu(gp_ref[:, cols].astype(F32))).astype(ypool_ref.dtype)


def _poolconv(proj, wpool, pscale, wdw, bdw, lng, lnb, wpw, *, layer, seq_len, rows, col0, width, name):
    m_rows = proj.shape[0]
    tiles_per_seq = seq_len // rows
    hpr = rows // HALO
    n_halo = m_rows // HALO
    cb = col0 // width

    def cur(k):
        return pl.BlockSpec((rows, width), lambda i: (i, cb + k))

    def prev(k):
        return pl.BlockSpec((HALO, width), lambda i: (jnp.maximum(i * hpr - 1, 0), cb + k))

    def nxt(k):
        return pl.BlockSpec((HALO, width), lambda i: (jnp.minimum((i + 1) * hpr, n_halo - 1), cb + k))

    full = lambda a: (pl.BlockSpec(a.shape, lambda i: (0, 0)) if a.ndim == 2 else
                      pl.BlockSpec((None,) + a.shape[1:], lambda i: (layer,) + (0,) * (a.ndim - 1)))
    consts = [wpool, pscale, wdw, bdw, lng, lnb, wpw]
    kern = functools.partial(_poolconv_kernel, layer=layer, rows=rows, tiles_per_seq=tiles_per_seq, seq_len=seq_len,
                             row_blk=min(64, rows))
    return pl.pallas_call(
        kern,
        grid=(m_rows // rows,),
        in_specs=[cur(0), prev(0), nxt(0), cur(1), cur(2), prev(2), nxt(2), cur(3), prev(3), nxt(3), cur(4)]
                 + [full(a) for a in consts],
        out_specs=[pl.BlockSpec((rows, width), lambda i: (i, 0)),
                   pl.BlockSpec((rows, width), lambda i: (i, 0))],
        out_shape=[jax.ShapeDtypeStruct((m_rows, width), BF16),
                   jax.ShapeDtypeStruct((m_rows, width), BF16)],
        scratch_shapes=[pltpu.VMEM((rows + 2 * HALO, width), F32),
                        pltpu.VMEM((rows + 2 * HALO, width), F32),
                        pltpu.VMEM((rows, width), BF16),
                        pltpu.VMEM((7, rows + 2 * HALO - 8, width), F32),
                        pltpu.VMEM((2, rows + 2 * HALO, POOL_GD), F32),
                        pltpu.VMEM((8 * CONV_K, width), F32)],
        compiler_params=_cparams(1),
        name=name,
    )(*([proj] * 11), *consts)


def _outproj_kernel(*refs, d_model, att_width, pool_width, final):
    if final:
        ya_ref, yp_ref, yc_ref, w_ref, x_ref, mod_ref, fg_ref, out_ref = refs
    else:
        ya_ref, yp_ref, yc_ref, w_ref, x_ref, mod_ref, out_ref = refs
    c1 = att_width
    c2 = att_width + pool_width
    y = jnp.dot(ya_ref[...], w_ref[0:c1, :].astype(BF16), preferred_element_type=F32)
    y = y + jnp.dot(yp_ref[...], w_ref[c1:c2, :].astype(BF16), preferred_element_type=F32)
    y = y + jnp.dot(yc_ref[...], w_ref[c2:, :].astype(BF16), preferred_element_type=F32)
    gate = mod_ref[0][:, 2 * d_model:]
    xn = x_ref[...] + gate * y
    if final:
        ms = jnp.mean(xn * xn, axis=-1, keepdims=True)
        xn = xn * lax.rsqrt(ms + EPS) * fg_ref[...]
    out_ref[...] = xn


def _outproj(ya, yp, yc, w, x2d, mod3, final_g, *, layer, tm, tiles_per_mod, mod_row0, name):
    m_rows, d = x2d.shape
    final = final_g is not None
    in_specs = [pl.BlockSpec((tm, ya.shape[1]), lambda i: (i, 0)),
                pl.BlockSpec((tm, yp.shape[1]), lambda i: (i, 0)),
                pl.BlockSpec((tm, yc.shape[1]), lambda i: (i, 0)),
                pl.BlockSpec((None,) + w.shape[1:], lambda i: (layer, 0, 0), pipeline_mode=pl.Buffered(1)),
                pl.BlockSpec((tm, d), lambda i: (i, 0)),
                pl.BlockSpec((1, 1, mod3.shape[-1]), lambda i: (8 * layer + mod_row0 + i // tiles_per_mod, 0, 0))]
    args = [ya, yp, yc, w, x2d, mod3]
    if final:
        in_specs.append(pl.BlockSpec((1, d), lambda i: (0, 0)))
        args.append(final_g)
    kern = functools.partial(_outproj_kernel, d_model=d, att_width=ya.shape[1], pool_width=yp.shape[1],
                             final=final)
    return pl.pallas_call(
        kern,
        grid=(m_rows // tm,),
        in_specs=in_specs,
        out_specs=pl.BlockSpec((tm, d), lambda i: (i, 0)),
        out_shape=jax.ShapeDtypeStruct((m_rows, d), F32),
        compiler_params=_cparams(1),
        name=name,
    )(*args)


def _rope_tables(seq_len):
    n_freq = ATT_HD // 4
    inv_freq = (np.float32(ROPE_BASE) ** (-np.arange(n_freq, dtype=np.float32) / n_freq)).astype(np.float32)
    t = np.arange(seq_len)
    lane = np.arange(LANES)
    in_col_half = (lane % ATT_HD) >= ATT_HD // 2
    pos = np.where(in_col_half[None, :], (t % GRID_W)[:, None], (t // GRID_W)[:, None]).astype(np.float32)
    ang = pos * inv_freq[lane % n_freq][None, :]
    cos = np.cos(ang).astype(np.float32)
    sin = np.sin(ang).astype(np.float32)
    qs = np.float32(ATT_HD ** -0.5 * math.log2(math.e))
    perm = np.zeros((LANES, LANES), np.float32)
    for p in range(LANES):
        if p % (2 * n_freq) < n_freq:
            perm[p + n_freq, p] = -1.0
        else:
            perm[p - n_freq, p] = 1.0
    return (jnp.asarray(np.stack([cos * qs, cos])), jnp.asarray(np.stack([sin * qs, sin])),
            jnp.asarray(perm, dtype=BF16))


def kernel(x, c, ctx, c_ctx, w_mod, b_mod, norm_g, w_in, lambda_q1, lambda_k1, lambda_q2, lambda_k2,
           subln_g, w_pool, pool_scale, w_dw, b_dw, conv_ln_g, conv_ln_b, w_pw2, w_out, final_g):
    batch, seq, d = x.shape
    lc = ctx.shape[1]
    depth = w_mod.shape[0]
    att_w = d // 2
    pool_w = d // 4
    assert batch < 8 and seq % GRID_W == 0 and att_w == ATT_HEADS * ATT_VD and pool_w == 4 * POOL_GD
    assert seq % KV_CHUNK == 0 and lc % KV_CHUNK == 0

    cc = jnp.concatenate([c, c_ctx[None, :], jnp.zeros((8 - batch - 1, d), F32)], axis=0)
    lam_init = [0.8 - 0.6 * math.exp(-0.3 * l) for l in range(depth)]
    li = jnp.broadcast_to(jnp.asarray(lam_init, F32)[:, None, None], (depth, 1, LANES))
    mod, lam = _modulation(cc, w_mod, b_mod, lambda_q1, lambda_k1, lambda_q2, lambda_k2, li)
    tabs = _rope_tables(seq)

    tm_lat = min(1024, seq)
    lat_group = math.gcd(batch * seq // tm_lat, 4)
    tn = 512
    n_in = w_in.shape[-1]
    hcol = att_w // LANES
    pc_col0 = 4 * att_w
    x2d = x.reshape(batch * seq, d)
    c2d = ctx.reshape(batch * lc, d)
    row = lambda a: a.reshape(1, -1)
    mod3 = mod.reshape(depth * 8, 1, 3 * d)
    lw = (w_pool, pool_scale, w_dw, b_dw, conv_ln_g, conv_ln_b, w_pw2)
    g_l = norm_g
    sg_l = subln_g

    for l in range(depth):
        last = l == depth - 1
        out_scale = 1.0 - lam_init[l]

        proj = _inproj(x2d, mod3, g_l, w_in, tabs, layer=l, tm=tm_lat, tn=tn, group=lat_group, tiles_per_mod=seq // tm_lat,
                       mod_row0=0, n0=0, n_tiles=n_in // tn, att_width=att_w, name=f"inproj_lat{l}")
        if last:
            cproj = _inproj(c2d, mod3, g_l, w_in, None, layer=l, tm=batch * lc, tn=tn, group=1, tiles_per_mod=1,
                            mod_row0=batch, n0=att_w // tn, n_tiles=2 * att_w // tn, att_width=att_w,
                            name=f"inproj_ctx{l}")
            kc_col, vc_col = 0, hcol
        else:
            cproj = _inproj(c2d, mod3, g_l, w_in, None, layer=l, tm=batch * lc, tn=tn, group=1, tiles_per_mod=1,
                            mod_row0=batch, n0=0, n_tiles=n_in // tn, att_width=att_w,
                            name=f"inproj_ctx{l}")
            kc_col, vc_col = hcol, 2 * hcol

        y_att = _attention(proj, cproj, proj, lam, sg_l, layer=l, batch=batch, lq=seq,
                           heads=LAT_HEADS_PER_STEP, q_col=0, g_col=3 * hcol, lc=lc, kc_col=kc_col, vc_col=vc_col,
                           ll=seq, kl_col=hcol, vl_col=2 * hcol, out_scale=out_scale, name=f"attn_lat{l}")
        y_pool, y_conv = _poolconv(proj, *lw, layer=l, seq_len=seq, rows=min(512, seq), col0=pc_col0, width=pool_w,
                                   name=f"poolconv_lat{l}")
        if not last:
            yc_att = _attention(cproj, cproj, None, lam, sg_l, layer=l, batch=batch, lq=lc,
                                heads=ATT_HEADS, q_col=0, g_col=3 * hcol, lc=lc, kc_col=kc_col, vc_col=vc_col,
                                ll=0, kl_col=0, vl_col=0, out_scale=out_scale, name=f"attn_ctx{l}")
            yc_pool, yc_conv = _poolconv(cproj, *lw, layer=l, seq_len=lc, rows=min(256, lc), col0=pc_col0,
                                         width=pool_w, name=f"poolconv_ctx{l}")
            c2d = _outproj(yc_att, yc_pool, yc_conv, w_out, c2d, mod3, None, layer=l, tm=min(512, batch * lc),
                           tiles_per_mod=batch * lc, mod_row0=batch, name=f"outproj_ctx{l}")
        tm_o = min(512, seq)
        x2d = _outproj(y_att, y_pool, y_conv, w_out, x2d, mod3, row(final_g) if last else None,
                       layer=l, tm=tm_o, tiles_per_mod=seq // tm_o, mod_row0=0, name=f"outproj_lat{l}")
    return x2d.reshape(batch, seq, d)
```

```python
import functools
import math

import jax
import jax.numpy as jnp
import numpy as np
from jax import lax
from jax.experimental import pallas as pl
from jax.experimental.pallas import tpu as pltpu

ATT_HEADS = 8
ATT_HD = 64
ATT_VD = 2 * ATT_HD
POOL_WINDOWS = (2, 4, 8, 16)
POOL_GD = 128
CONV_K = 31
GRID_W = 64
ROPE_BASE = 10000.0
EPS = 1e-6

LANES = 128
KV_CHUNK = 256
Q_SUB = 128
LAT_HEADS_PER_STEP = 2
SM_GROUP = 32
SM_AHEAD = 2
HALO = 16
VMEM_LIMIT = 56 * 1024 * 1024

F32 = jnp.float32
BF16 = jnp.bfloat16


def _cparams(n_axes):
    return pltpu.CompilerParams(dimension_semantics=("arbitrary",) * n_axes, vmem_limit_bytes=VMEM_LIMIT)


def _silu(v):
    return v * jax.nn.sigmoid(v)


def _mod_kernel(cc_ref, w_ref, b_ref, lq1_ref, lk1_ref, lq2_ref, lk2_ref, li_ref, out_ref, lam_ref):
    s = _silu(cc_ref[...])
    out_ref[0] = jnp.dot(s.astype(BF16), w_ref[0].astype(BF16), preferred_element_type=F32) + b_ref[0]
    a1 = jnp.sum(lq1_ref[0] * lk1_ref[0], axis=-1, keepdims=True)
    a2 = jnp.sum(lq2_ref[0] * lk2_ref[0], axis=-1, keepdims=True)
    lam = jnp.exp(a1) - jnp.exp(a2) + li_ref[0]
    lam_ref[0] = jnp.broadcast_to(lam, lam_ref.shape[1:])


def _modulation(cc, w_mod, b_mod, lq1, lk1, lq2, lk2, lam_init):
    depth, d, n = w_mod.shape
    tn = n // 4 if n % (4 * LANES) == 0 else n
    vec = lambda a: a.reshape(depth, 1, a.shape[-1])
    lspec = pl.BlockSpec((1, 1, ATT_HD), lambda l, j: (l, 0, 0))
    return pl.pallas_call(
        _mod_kernel,
        grid=(depth, n // tn),
        in_specs=[pl.BlockSpec((8, d), lambda l, j: (0, 0)),
                  pl.BlockSpec((1, d, tn), lambda l, j: (l, 0, j)),
                  pl.BlockSpec((1, 1, tn), lambda l, j: (l, 0, j)),
                  lspec, lspec, lspec, lspec,
                  pl.BlockSpec((1, 1, LANES), lambda l, j: (l, 0, 0))],
        out_specs=[pl.BlockSpec((1, 8, tn), lambda l, j: (l, 0, j)),
                   pl.BlockSpec((1, 8, LANES), lambda l, j: (l, 0, 0))],
        out_shape=[jax.ShapeDtypeStruct((depth, 8, n), F32),
                   jax.ShapeDtypeStruct((depth, 8, LANES), F32)],
        compiler_params=_cparams(2),
        name="modulation",
    )(cc, w_mod, vec(b_mod), vec(lq1), vec(lk1), vec(lq2), vec(lk2), lam_init)


def _inproj_kernel(*refs, layer, d_model, rope, n_rope_tiles, n_q_tiles, row_chunk):
    if rope:
        x_ref, mod_ref, g_ref, w_ref, cos_ref, sin_ref, perm_ref, out_ref, h_ref = refs
    else:
        x_ref, mod_ref, g_ref, w_ref, out_ref, h_ref = refs
    j = pl.program_id(1)
    tm, tn = out_ref.shape
    row0 = pl.multiple_of(pl.program_id(2) * tm, tm)

    def rope_store(rows, acc):
        cos = cos_ref[rows, :]
        sin = sin_ref[rows, :]
        cos2 = jnp.concatenate([cos, cos], axis=1)
        sin2 = jnp.concatenate([sin, sin], axis=1)
        for hh in range(tn // (2 * LANES)):
            cols = slice(2 * hh * LANES, 2 * (hh + 1) * LANES)
            t = acc[:, cols]
            rot = jnp.dot(t.astype(BF16), perm_ref[...], preferred_element_type=F32)
            out_ref[rows, cols] = (t * cos2 + rot * sin2).astype(out_ref.dtype)

    def plain_store(rows, acc):
        qs = jnp.where(j < n_q_tiles, ATT_HD ** -0.5 * math.log2(math.e), 1.0).astype(F32)
        out_ref[rows, :] = (acc * qs).astype(out_ref.dtype)

    @pl.when(j == 0)
    def _():
        m = mod_ref[0]
        shift = m[:, :d_model]
        gain = g_ref[layer:layer + 1, :] * (1.0 + m[:, d_model:2 * d_model])
        wb = w_ref[...].astype(BF16)
        for c in range(tm // row_chunk):
            rows = slice(c * row_chunk, (c + 1) * row_chunk)
            xf = x_ref[rows, :]
            ms = jnp.mean(xf * xf, axis=-1, keepdims=True)
            h = (xf * lax.rsqrt(ms + EPS) * gain + shift).astype(BF16)
            h_ref[pl.ds(row0 + c * row_chunk, row_chunk), :] = h
            acc = jnp.dot(h, wb, preferred_element_type=F32)
            if rope:
                rope_store(rows, acc)
            else:
                plain_store(rows, acc)

    def later_tile():
        return jnp.dot(h_ref[pl.ds(row0, tm), :], w_ref[...].astype(BF16), preferred_element_type=F32)

    if rope and n_rope_tiles > 1:
        @pl.when(jnp.logical_and(j > 0, j < n_rope_tiles))
        def _():
            rope_store(slice(None), later_tile())

    @pl.when(j >= (n_rope_tiles if rope else 1))
    def _():
        plain_store(slice(None), later_tile())


def _inproj(x2d, mod3, g, w, tabs, *, layer, tm, tn, group, tiles_per_mod, mod_row0, n0, n_tiles, att_width,
            name):
    m_rows, d = x2d.shape
    rope = tabs is not None
    n_q_tiles = max(att_width // tn - n0, 0)
    n_rope_tiles = max(2 * att_width // tn - n0, 0)
    tile = lambda gi, i: gi * group + i
    in_specs = [pl.BlockSpec((tm, d), lambda gi, j, i: (jnp.where(j == 0, tile(gi, i), tile(gi, group - 1)), 0)),
                pl.BlockSpec((1, 1, mod3.shape[-1]),
                             lambda gi, j, i: (8 * layer + mod_row0 + tile(gi, i) // tiles_per_mod, 0, 0)),
                pl.BlockSpec(g.shape, lambda gi, j, i: (0, 0)),
                pl.BlockSpec((None, d, tn), lambda gi, j, i: (layer, 0, n0 + j))]
    args = [x2d, mod3, g, w]
    if rope:
        tiles_per_seq = tabs[0].shape[1] // tm
        tspec = pl.BlockSpec((None, tm, LANES),
                             lambda gi, j, i: (jnp.where(j < n_q_tiles, 0, 1),
                                               jnp.where(j < n_rope_tiles, tile(gi, i) % tiles_per_seq, 0), 0))
        in_specs += [tspec, tspec, pl.BlockSpec((2 * LANES, 2 * LANES), lambda gi, j, i: (0, 0))]
        args += list(tabs)
    kern = functools.partial(_inproj_kernel, layer=layer, d_model=d, rope=rope, n_rope_tiles=n_rope_tiles,
                             n_q_tiles=n_q_tiles, row_chunk=min(256, tm))
    return pl.pallas_call(
        kern,
        grid=(m_rows // (tm * group), n_tiles, group),
        in_specs=in_specs,
        out_specs=pl.BlockSpec((tm, tn), lambda gi, j, i: (tile(gi, i), j)),
        out_shape=jax.ShapeDtypeStruct((m_rows, n_tiles * tn), BF16),
        scratch_shapes=[pltpu.VMEM((group * tm, d), BF16)],
        compiler_params=_cparams(3),
        name=name,
    )(*args)


def _attn_kernel(*refs, layer, lc, ll, n_sub, heads, out_scale):
    if ll:
        q_ref, kc_ref, vc_ref, kl_ref, vl_ref, g_ref, lam_ref, sg_ref, out_ref, vt_ref, *slots = refs
        k_parts = [(kc_ref, 0, lc), (kl_ref, lc, ll)]
        v_parts = [(vc_ref, 0, lc), (vl_ref, lc, ll)]
    else:
        q_ref, kc_ref, vc_ref, g_ref, lam_ref, sg_ref, out_ref, vt_ref, *slots = refs
        k_parts = [(kc_ref, 0, lc)]
        v_parts = [(vc_ref, 0, lc)]
    lk = lc + ll
    st_ref = slots[0:2]
    pt_ref = slots[2:4]

    def head_lanes(hh):
        return slice(hh * LANES, (hh + 1) * LANES)

    for hh in range(heads):
        for ref, off, n in v_parts:
            for c in range(n // KV_CHUNK):
                blk = ref[c * KV_CHUNK:(c + 1) * KV_CHUNK, head_lanes(hh)].astype(F32)
                vt_ref[hh, :, off + c * KV_CHUNK:off + (c + 1) * KV_CHUNK] = blk.T.astype(BF16)

    row = lax.broadcasted_iota(jnp.int32, (ATT_VD, Q_SUB), 0)
    lam = lam_ref[0][0:1, :]
    sg = sg_ref[layer:layer + 1, :]

    def item(t):
        return t // n_sub, pl.ds((t % n_sub) * Q_SUB, Q_SUB)

    def scores(t, slot):
        hh, q_rows = item(t)
        qt = q_ref[q_rows, head_lanes(hh)].astype(F32).T
        qbd = jnp.concatenate([jnp.where(row < ATT_HD, qt, 0.0),
                               jnp.where(row >= ATT_HD, qt, 0.0)], axis=1).astype(BF16)
        m8 = None
        for k_ref, off, n in k_parts:
            for c in range(n // KV_CHUNK):
                kch = k_ref[c * KV_CHUNK:(c + 1) * KV_CHUNK, head_lanes(hh)]
                st = jnp.dot(kch, qbd, preferred_element_type=F32)
                st_ref[slot][off + c * KV_CHUNK:off + (c + 1) * KV_CHUNK, :] = st
                for r in range(KV_CHUNK // 8):
                    piece = st[r * 8:(r + 1) * 8, :]
                    m8 = piece if m8 is None else jnp.maximum(m8, piece)
        return jnp.max(m8, axis=0, keepdims=True)

    def softmax(slot, m):
        l8 = None
        marks = []
        m8 = jnp.broadcast_to(m, (8, 2 * Q_SUB))
        for g in range(lk // SM_GROUP):
            rows = slice(g * SM_GROUP, (g + 1) * SM_GROUP)
            m_use = m8 if g < SM_AHEAD else m8 + marks[g - SM_AHEAD] * 0.0
            p = jnp.exp2(st_ref[slot][rows, :] - jnp.tile(m_use, (SM_GROUP // 8, 1)))
            for r in range(SM_GROUP // 8):
                piece = p[r * 8:(r + 1) * 8, :]
                l8 = piece if l8 is None else l8 + piece
            marks.append(l8)
            pt_ref[slot][rows, :] = p.astype(BF16)
        return l8

    def pv(t, slot, l8):
        hh, q_rows = item(t)
        acc = jnp.dot(vt_ref[hh], pt_ref[slot][...], preferred_element_type=F32)
        o = acc * (1.0 / jnp.sum(l8, axis=0, keepdims=True))
        ot = o[:, :Q_SUB] - lam * o[:, Q_SUB:]
        oq = ot.T
        ms = jnp.mean(oq * oq, axis=-1, keepdims=True)
        y = oq * lax.rsqrt(ms + EPS) * sg * out_scale
        gate = g_ref[q_rows, head_lanes(hh)].astype(F32)
        out_ref[q_rows, head_lanes(hh)] = (y * _silu(gate)).astype(out_ref.dtype)

    def tick(t, slot, carry, do_scores=True, do_softmax=True, do_pv=True):
        m_prev, l8_prev = carry
        m = scores(t, slot) if do_scores else None
        l8 = softmax(1 - slot, m_prev) if do_softmax else None
        if do_pv:
            pv(t - 2, slot, l8_prev)
        return m, l8

    n_items = heads * n_sub
    carry = (None, None)
    for t in range(n_items + 2):
        carry = tick(t, t % 2, carry, do_scores=t < n_items, do_softmax=1 <= t <= n_items, do_pv=t >= 2)


def _attention(q_arr, kv_ctx, kv_lat, lam, subln_g, *, layer, batch, lq, heads, q_col, g_col, lc, kc_col, vc_col,
               ll, kl_col, vl_col, out_scale, name):
    width = heads * LANES
    assert ATT_HEADS % heads == 0 and all(c % heads == 0 for c in (q_col, g_col, kc_col, vc_col, kl_col, vl_col))
    col = lambda c: (lambda b, h: (b, c // heads + h))
    in_specs = [pl.BlockSpec((lq, width), col(q_col)),
                pl.BlockSpec((lc, width), col(kc_col)),
                pl.BlockSpec((lc, width), col(vc_col))]
    args = [q_arr, kv_ctx, kv_ctx]
    if kv_lat is not None:
        in_specs += [pl.BlockSpec((ll, width), col(kl_col)),
                     pl.BlockSpec((ll, width), col(vl_col))]
        args += [kv_lat, kv_lat]
    else:
        ll = 0
    in_specs += [pl.BlockSpec((lq, width), col(g_col)),
                 pl.BlockSpec((1, 8, LANES), lambda b, h: (layer, 0, 0)),
                 pl.BlockSpec(subln_g.shape, lambda b, h: (0, 0))]
    args += [q_arr, lam, subln_g]
    kern = functools.partial(_attn_kernel, layer=layer, lc=lc, ll=ll, n_sub=lq // Q_SUB, heads=heads, out_scale=out_scale)
    return pl.pallas_call(
        kern,
        grid=(batch, ATT_HEADS // heads),
        in_specs=in_specs,
        out_specs=pl.BlockSpec((lq, width), lambda b, h: (b, h)),
        out_shape=jax.ShapeDtypeStruct((batch * lq, ATT_HEADS * ATT_VD), BF16),
        scratch_shapes=[pltpu.VMEM((heads, ATT_VD, lc + ll), BF16),
                        pltpu.VMEM((lc + ll, 2 * Q_SUB), F32), pltpu.VMEM((lc + ll, 2 * Q_SUB), F32),
                        pltpu.VMEM((lc + ll, 2 * Q_SUB), BF16), pltpu.VMEM((lc + ll, 2 * Q_SUB), BF16)],
        compiler_params=_cparams(2),
        name=name,
    )(*args)


def _poolconv_kernel(up_c, up_p, up_n, gp_ref, a_c, a_p, a_n, b_c, b_p, b_n, gc_ref,
                     wpool_ref, pscale_ref, wdw_ref, bdw_ref, lng_ref, lnb_ref, wpw_ref,
                     ypool_ref, yconv_ref, u_ref, p_ref, z_ref, s_ref, r_ref, w8_ref, *, layer, rows, tiles_per_seq,
                     seq_len, row_blk):
    iseq = pl.program_id(0) % tiles_per_seq
    has_prev = iseq > 0
    has_next = iseq < tiles_per_seq - 1

    def glu(a_ref, b_ref):
        return a_ref[...].astype(F32) * jax.nn.sigmoid(b_ref[...].astype(F32))

    u_ref[0:HALO, :] = jnp.where(has_prev, glu(a_p, b_p), 0.0)
    u_ref[HALO:HALO + rows, :] = glu(a_c, b_c)
    u_ref[HALO + rows:, :] = jnp.where(has_next, glu(a_n, b_n), 0.0)
    p_ref[0:HALO, :] = jnp.where(has_prev, up_p[...].astype(F32), 0.0)
    p_ref[HALO:HALO + rows, :] = up_c[...].astype(F32)
    p_ref[HALO + rows:, :] = jnp.where(has_next, up_n[...].astype(F32), 0.0)

    ext = rows + 2 * HALO - 8
    for b in range(1, 8):
        s_ref[b - 1] = u_ref[b:b + ext, :]

    width = u_ref.shape[1]

    @pl.when(pl.program_id(0) == 0)
    def _():
        for j in range(CONV_K):
            w8_ref[8 * j:8 * j + 8, :] = jnp.broadcast_to(wdw_ref[j:j + 1, :], (8, width))

    lrow = slice(layer, layer + 1)
    for rb in range(rows // row_blk):
        acc = None
        for j in range(CONV_K):
            off = HALO - CONV_K // 2 + j
            base = rb * row_blk + (off // 8) * 8
            if off % 8 == 0:
                tap = u_ref[base:base + row_blk, :]
            else:
                tap = s_ref[off % 8 - 1, base:base + row_blk, :]
            term = (tap.reshape(row_blk // 8, 8, width) * w8_ref[8 * j:8 * j + 8, :]).reshape(row_blk, width)
            acc = term if acc is None else acc + term
        y = acc + bdw_ref[lrow, :]
        mu = jnp.mean(y, axis=-1, keepdims=True)
        yc = y - mu
        var = jnp.mean(yc * yc, axis=-1, keepdims=True)
        z = yc * lax.rsqrt(var + EPS) * lng_ref[lrow, :] + lnb_ref[lrow, :]
        z_ref[rb * row_blk:(rb + 1) * row_blk, :] = _silu(z).astype(BF16)
    yconv = jnp.dot(z_ref[...], wpw_ref[...].astype(BF16), preferred_element_type=F32)
    yconv_ref[...] = (yconv * _silu(gc_ref[...].astype(F32))).astype(yconv_ref.dtype)

    t = iseq * rows + lax.broadcasted_iota(jnp.int32, (rows, POOL_GD), 0)
    for g, w in enumerate(POOL_WINDOWS):
        cols = slice(g * POOL_GD, (g + 1) * POOL_GD)
        hw = w // 2
        load = lambda s, m, _c=cols: p_ref[s:s + m, _c]
        n, level = 1, 0
        while n < hw:
            dst = r_ref.at[level % 2]
            dst[8:rows + HALO + 8, :] = load(8, rows + HALO) + load(8 + n, rows + HALO)
            dst[rows + HALO + 8:, :] = jnp.zeros((8, POOL_GD), F32)
            load = lambda s, m, _d=dst: _d[s:s + m, :]
            n, level = 2 * n, level + 1
        ssum = load(HALO - hw, rows) + load(HALO, rows)
        cnt = (jnp.minimum(t + hw, seq_len) - jnp.maximum(t - hw, 0)).astype(F32)
        dlt = ssum / cnt - p_ref[HALO:HALO + rows, cols]
        yp = jnp.dot(dlt.astype(BF16), wpool_ref[g].astype(BF16), preferred_element_type=F32) * pscale_ref[lrow, cols]
        ypool_ref[:, cols] = (yp * _silu(gp_ref[:, cols].astype(F32))).astype(ypool_ref.dtype)


def _poolconv(proj, wpool, pscale, wdw, bdw, lng, lnb, wpw, *, layer, seq_len, rows, col0, width, name):
    m_rows = proj.shape[0]
    tiles_per_seq = seq_len // rows
    hpr = rows // HALO
    n_halo = m_rows // HALO
    cb = col0 // width

    def cur(k):
        return pl.BlockSpec((rows, width), lambda i: (i, cb + k))

    def prev(k):
        return pl.BlockSpec((HALO, width), lambda i: (jnp.maximum(i * hpr - 1, 0), cb + k))

    def nxt(k):
        return pl.BlockSpec((HALO, width), lambda i: (jnp.minimum((i + 1) * hpr, n_halo - 1), cb + k))

    full = lambda a: (pl.BlockSpec(a.shape, lambda i: (0, 0)) if a.ndim == 2 else
                      pl.BlockSpec((None,) + a.shape[1:], lambda i: (layer,) + (0,) * (a.ndim - 1)))
    consts = [wpool, pscale, wdw, bdw, lng, lnb, wpw]
    kern = functools.partial(_poolconv_kernel, layer=layer, rows=rows, tiles_per_seq=tiles_per_seq, seq_len=seq_len,
                             row_blk=min(32, rows))
    return pl.pallas_call(
        kern,
        grid=(m_rows // rows,),
        in_specs=[cur(0), prev(0), nxt(0), cur(1), cur(2), prev(2), nxt(2), cur(3), prev(3), nxt(3), cur(4)]
                 + [full(a) for a in consts],
        out_specs=[pl.BlockSpec((rows, width), lambda i: (i, 0)),
                   pl.BlockSpec((rows, width), lambda i: (i, 0))],
        out_shape=[jax.ShapeDtypeStruct((m_rows, width), BF16),
                   jax.ShapeDtypeStruct((m_rows, width), BF16)],
        scratch_shapes=[pltpu.VMEM((rows + 2 * HALO, width), F32),
                        pltpu.VMEM((rows + 2 * HALO, width), F32),
                        pltpu.VMEM((rows, width), BF16),
                        pltpu.VMEM((7, rows + 2 * HALO - 8, width), F32),
                        pltpu.VMEM((2, rows + 2 * HALO, POOL_GD), F32),
                        pltpu.VMEM((8 * CONV_K, width), F32)],
        compiler_params=_cparams(1),
        name=name,
    )(*([proj] * 11), *consts)


def _outproj_kernel(*refs, d_model, att_width, pool_width, final):
    if final:
        ya_ref, yp_ref, yc_ref, w_ref, x_ref, mod_ref, fg_ref, out_ref = refs
    else:
        ya_ref, yp_ref, yc_ref, w_ref, x_ref, mod_ref, out_ref = refs
    c1 = att_width
    c2 = att_width + pool_width
    y = jnp.dot(ya_ref[...], w_ref[0:c1, :].astype(BF16), preferred_element_type=F32)
    y = y + jnp.dot(yp_ref[...], w_ref[c1:c2, :].astype(BF16), preferred_element_type=F32)
    y = y + jnp.dot(yc_ref[...], w_ref[c2:, :].astype(BF16), preferred_element_type=F32)
    gate = mod_ref[0][:, 2 * d_model:]
    xn = x_ref[...] + gate * y
    if final:
        ms = jnp.mean(xn * xn, axis=-1, keepdims=True)
        xn = xn * lax.rsqrt(ms + EPS) * fg_ref[...]
    out_ref[...] = xn


def _outproj(ya, yp, yc, w, x2d, mod3, final_g, *, layer, tm, tiles_per_mod, mod_row0, name):
    m_rows, d = x2d.shape
    final = final_g is not None
    in_specs = [pl.BlockSpec((tm, ya.shape[1]), lambda i: (i, 0)),
                pl.BlockSpec((tm, yp.shape[1]), lambda i: (i, 0)),
                pl.BlockSpec((tm, yc.shape[1]), lambda i: (i, 0)),
                pl.BlockSpec((None,) + w.shape[1:], lambda i: (layer, 0, 0), pipeline_mode=pl.Buffered(1)),
                pl.BlockSpec((tm, d), lambda i: (i, 0)),
                pl.BlockSpec((1, 1, mod3.shape[-1]), lambda i: (8 * layer + mod_row0 + i // tiles_per_mod, 0, 0))]
    args = [ya, yp, yc, w, x2d, mod3]
    if final:
        in_specs.append(pl.BlockSpec((1, d), lambda i: (0, 0)))
        args.append(final_g)
    kern = functools.partial(_outproj_kernel, d_model=d, att_width=ya.shape[1], pool_width=yp.shape[1],
                             final=final)
    return pl.pallas_call(
        kern,
        grid=(m_rows // tm,),
        in_specs=in_specs,
        out_specs=pl.BlockSpec((tm, d), lambda i: (i, 0)),
        out_shape=jax.ShapeDtypeStruct((m_rows, d), F32),
        compiler_params=_cparams(1),
        name=name,
    )(*args)


def _rope_tables(seq_len):
    n_freq = ATT_HD // 4
    inv_freq = (np.float32(ROPE_BASE) ** (-np.arange(n_freq, dtype=np.float32) / n_freq)).astype(np.float32)
    t = np.arange(seq_len)
    lane = np.arange(LANES)
    in_col_half = (lane % ATT_HD) >= ATT_HD // 2
    pos = np.where(in_col_half[None, :], (t % GRID_W)[:, None], (t // GRID_W)[:, None]).astype(np.float32)
    ang = pos * inv_freq[lane % n_freq][None, :]
    cos = np.cos(ang).astype(np.float32)
    sin = np.sin(ang).astype(np.float32)
    qs = np.float32(ATT_HD ** -0.5 * math.log2(math.e))
    perm = np.zeros((2 * LANES, 2 * LANES), np.float32)
    for p in range(2 * LANES):
        if p % (2 * n_freq) < n_freq:
            perm[p + n_freq, p] = -1.0
        else:
            perm[p - n_freq, p] = 1.0
    return (jnp.asarray(np.stack([cos * qs, cos])), jnp.asarray(np.stack([sin * qs, sin])),
            jnp.asarray(perm, dtype=BF16))


def kernel(x, c, ctx, c_ctx, w_mod, b_mod, norm_g, w_in, lambda_q1, lambda_k1, lambda_q2, lambda_k2,
           subln_g, w_pool, pool_scale, w_dw, b_dw, conv_ln_g, conv_ln_b, w_pw2, w_out, final_g):
    batch, seq, d = x.shape
    lc = ctx.shape[1]
    depth = w_mod.shape[0]
    att_w = d // 2
    pool_w = d // 4
    assert batch < 8 and seq % GRID_W == 0 and att_w == ATT_HEADS * ATT_VD and pool_w == 4 * POOL_GD
    assert seq % KV_CHUNK == 0 and lc % KV_CHUNK == 0

    cc = jnp.concatenate([c, c_ctx[None, :], jnp.zeros((8 - batch - 1, d), F32)], axis=0)
    lam_init = [0.8 - 0.6 * math.exp(-0.3 * l) for l in range(depth)]
    li = jnp.broadcast_to(jnp.asarray(lam_init, F32)[:, None, None], (depth, 1, LANES))
    mod, lam = _modulation(cc, w_mod, b_mod, lambda_q1, lambda_k1, lambda_q2, lambda_k2, li)
    tabs = _rope_tables(seq)

    tm_lat = min(1024, seq)
    lat_group = math.gcd(batch * seq // tm_lat, 4)
    tn = 512
    n_in = w_in.shape[-1]
    hcol = att_w // LANES
    pc_col0 = 4 * att_w
    x2d = x.reshape(batch * seq, d)
    c2d = ctx.reshape(batch * lc, d)
    row = lambda a: a.reshape(1, -1)
    mod3 = mod.reshape(depth * 8, 1, 3 * d)
    lw = (w_pool, pool_scale, w_dw, b_dw, conv_ln_g, conv_ln_b, w_pw2)
    g_l = norm_g
    sg_l = subln_g

    for l in range(depth):
        last = l == depth - 1
        out_scale = 1.0 - lam_init[l]

        proj = _inproj(x2d, mod3, g_l, w_in, tabs, layer=l, tm=tm_lat, tn=tn, group=lat_group, tiles_per_mod=seq // tm_lat,
                       mod_row0=0, n0=0, n_tiles=n_in // tn, att_width=att_w, name=f"inproj_lat{l}")
        if last:
            cproj = _inproj(c2d, mod3, g_l, w_in, None, layer=l, tm=batch * lc, tn=tn, group=1, tiles_per_mod=1,
                            mod_row0=batch, n0=att_w // tn, n_tiles=2 * att_w // tn, att_width=att_w,
                            name=f"inproj_ctx{l}")
            kc_col, vc_col = 0, hcol
        else:
            cproj = _inproj(c2d, mod3, g_l, w_in, None, layer=l, tm=batch * lc, tn=tn, group=1, tiles_per_mod=1,
                            mod_row0=batch, n0=0, n_tiles=n_in // tn, att_width=att_w,
                            name=f"inproj_ctx{l}")
            kc_col, vc_col = hcol, 2 * hcol

        y_att = _attention(proj, cproj, proj, lam, sg_l, layer=l, batch=batch, lq=seq,
                           heads=LAT_HEADS_PER_STEP, q_col=0, g_col=3 * hcol, lc=lc, kc_col=kc_col, vc_col=vc_col,
                           ll=seq, kl_col=hcol, vl_col=2 * hcol, out_scale=out_scale, name=f"attn_lat{l}")
        y_pool, y_conv = _poolconv(proj, *lw, layer=l, seq_len=seq, rows=min(1024, seq), col0=pc_col0, width=pool_w,
                                   name=f"poolconv_lat{l}")
        if not last:
            yc_att = _attention(cproj, cproj, None, lam, sg_l, layer=l, batch=batch, lq=lc,
                                heads=ATT_HEADS, q_col=0, g_col=3 * hcol, lc=lc, kc_col=kc_col, vc_col=vc_col,
                                ll=0, kl_col=0, vl_col=0, out_scale=out_scale, name=f"attn_ctx{l}")
            yc_pool, yc_conv = _poolconv(cproj, *lw, layer=l, seq_len=lc, rows=min(256, lc), col0=pc_col0,
                                         width=pool_w, name=f"poolconv_ctx{l}")
            c2d = _outproj(yc_att, yc_pool, yc_conv, w_out, c2d, mod3, None, layer=l, tm=min(512, batch * lc),
                           tiles_per_mod=batch * lc, mod_row0=batch, name=f"outproj_ctx{l}")
        tm_o = min(512, seq)
        x2d = _outproj(y_att, y_pool, y_conv, w_out, x2d, mod3, row(final_g) if last else None,
                       layer=l, tm=tm_o, tiles_per_mod=seq // tm_o, mod_row0=0, name=f"outproj_lat{l}")
    return x2d.reshape(batch, seq, d)
```

```python
import functools
import math

import jax
import jax.numpy as jnp
import numpy as np
from jax import lax
from jax.experimental import pallas as pl
from jax.experimental.pallas import tpu as pltpu

ATT_HEADS = 8
ATT_HD = 64
ATT_VD = 2 * ATT_HD
POOL_WINDOWS = (2, 4, 8, 16)
POOL_GD = 128
CONV_K = 31
GRID_W = 64
ROPE_BASE = 10000.0
EPS = 1e-6

LANES = 128
KV_CHUNK = 256
Q_SUB = 128
LAT_HEADS_PER_STEP = 2
SM_GROUP = 32
SM_AHEAD = 2
HALO = 16
VMEM_LIMIT = 56 * 1024 * 1024

F32 = jnp.float32
BF16 = jnp.bfloat16


def _cparams(n_axes):
    return pltpu.CompilerParams(dimension_semantics=("arbitrary",) * n_axes, vmem_limit_bytes=VMEM_LIMIT)


def _silu(v):
    return v * jax.nn.sigmoid(v)


def _mod_kernel(cc_ref, w_ref, b_ref, lq1_ref, lk1_ref, lq2_ref, lk2_ref, li_ref, out_ref, lam_ref):
    s = _silu(cc_ref[...])
    out_ref[0] = jnp.dot(s.astype(BF16), w_ref[0].astype(BF16), preferred_element_type=F32) + b_ref[0]
    a1 = jnp.sum(lq1_ref[0] * lk1_ref[0], axis=-1, keepdims=True)
    a2 = jnp.sum(lq2_ref[0] * lk2_ref[0], axis=-1, keepdims=True)
    lam = jnp.exp(a1) - jnp.exp(a2) + li_ref[0]
    lam_ref[0] = jnp.broadcast_to(lam, lam_ref.shape[1:])


def _modulation(cc, w_mod, b_mod, lq1, lk1, lq2, lk2, lam_init):
    depth, d, n = w_mod.shape
    tn = n // 4 if n % (4 * LANES) == 0 else n
    vec = lambda a: a.reshape(depth, 1, a.shape[-1])
    lspec = pl.BlockSpec((1, 1, ATT_HD), lambda l, j: (l, 0, 0))
    return pl.pallas_call(
        _mod_kernel,
        grid=(depth, n // tn),
        in_specs=[pl.BlockSpec((8, d), lambda l, j: (0, 0)),
                  pl.BlockSpec((1, d, tn), lambda l, j: (l, 0, j)),
                  pl.BlockSpec((1, 1, tn), lambda l, j: (l, 0, j)),
                  lspec, lspec, lspec, lspec,
                  pl.BlockSpec((1, 1, LANES), lambda l, j: (l, 0, 0))],
        out_specs=[pl.BlockSpec((1, 8, tn), lambda l, j: (l, 0, j)),
                   pl.BlockSpec((1, 8, LANES), lambda l, j: (l, 0, 0))],
        out_shape=[jax.ShapeDtypeStruct((depth, 8, n), F32),
                   jax.ShapeDtypeStruct((depth, 8, LANES), F32)],
        compiler_params=_cparams(2),
        name="modulation",
    )(cc, w_mod, vec(b_mod), vec(lq1), vec(lk1), vec(lq2), vec(lk2), lam_init)


def _inproj_kernel(*refs, layer, d_model, rope, n_rope_tiles, n_q_tiles, row_chunk):
    if rope:
        x_ref, mod_ref, g_ref, w_ref, cos_ref, sin_ref, perm_ref, out_ref, h_ref = refs
    else:
        x_ref, mod_ref, g_ref, w_ref, out_ref, h_ref = refs
    j = pl.program_id(1)
    tm, tn = out_ref.shape
    row0 = pl.multiple_of(pl.program_id(2) * tm, tm)

    def rope_store(rows, acc):
        cos = cos_ref[rows, :]
        sin = sin_ref[rows, :]
        cos2 = jnp.concatenate([cos, cos], axis=1)
        sin2 = jnp.concatenate([sin, sin], axis=1)
        for hh in range(tn // (2 * LANES)):
            cols = slice(2 * hh * LANES, 2 * (hh + 1) * LANES)
            t = acc[:, cols]
            rot = jnp.dot(t.astype(BF16), perm_ref[...], preferred_element_type=F32)
            out_ref[rows, cols] = (t * cos2 + rot * sin2).astype(out_ref.dtype)

    def plain_store(rows, acc):
        qs = jnp.where(j < n_q_tiles, ATT_HD ** -0.5 * math.log2(math.e), 1.0).astype(F32)
        out_ref[rows, :] = (acc * qs).astype(out_ref.dtype)

    @pl.when(j == 0)
    def _():
        m = mod_ref[0]
        shift = m[:, :d_model]
        gain = g_ref[layer:layer + 1, :] * (1.0 + m[:, d_model:2 * d_model])
        wb = w_ref[...].astype(BF16)
        for c in range(tm // row_chunk):
            rows = slice(c * row_chunk, (c + 1) * row_chunk)
            xf = x_ref[rows, :]
            ms = jnp.mean(xf * xf, axis=-1, keepdims=True)
            h = (xf * lax.rsqrt(ms + EPS) * gain + shift).astype(BF16)
            h_ref[pl.ds(row0 + c * row_chunk, row_chunk), :] = h
            acc = jnp.dot(h, wb, preferred_element_type=F32)
            if rope:
                rope_store(rows, acc)
            else:
                plain_store(rows, acc)

    def later_tile():
        return jnp.dot(h_ref[pl.ds(row0, tm), :], w_ref[...].astype(BF16), preferred_element_type=F32)

    if rope and n_rope_tiles > 1:
        @pl.when(jnp.logical_and(j > 0, j < n_rope_tiles))
        def _():
            rope_store(slice(None), later_tile())

    @pl.when(j >= (n_rope_tiles if rope else 1))
    def _():
        plain_store(slice(None), later_tile())


def _inproj(x2d, mod3, g, w, tabs, *, layer, tm, tn, group, tiles_per_mod, mod_row0, n0, n_tiles, att_width,
            name):
    m_rows, d = x2d.shape
    rope = tabs is not None
    n_q_tiles = max(att_width // tn - n0, 0)
    n_rope_tiles = max(2 * att_width // tn - n0, 0)
    tile = lambda gi, i: gi * group + i
    in_specs = [pl.BlockSpec((tm, d), lambda gi, j, i: (jnp.where(j == 0, tile(gi, i), tile(gi, group - 1)), 0)),
                pl.BlockSpec((1, 1, mod3.shape[-1]),
                             lambda gi, j, i: (8 * layer + mod_row0 + tile(gi, i) // tiles_per_mod, 0, 0)),
                pl.BlockSpec(g.shape, lambda gi, j, i: (0, 0)),
                pl.BlockSpec((None, d, tn), lambda gi, j, i: (layer, 0, n0 + j))]
    args = [x2d, mod3, g, w]
    if rope:
        tiles_per_seq = tabs[0].shape[1] // tm
        tspec = pl.BlockSpec((None, tm, LANES),
                             lambda gi, j, i: (jnp.where(j < n_q_tiles, 0, 1),
                                               jnp.where(j < n_rope_tiles, tile(gi, i) % tiles_per_seq, 0), 0))
        in_specs += [tspec, tspec, pl.BlockSpec((2 * LANES, 2 * LANES), lambda gi, j, i: (0, 0))]
        args += list(tabs)
    kern = functools.partial(_inproj_kernel, layer=layer, d_model=d, rope=rope, n_rope_tiles=n_rope_tiles,
                             n_q_tiles=n_q_tiles, row_chunk=min(256, tm))
    return pl.pallas_call(
        kern,
        grid=(m_rows // (tm * group), n_tiles, group),
        in_specs=in_specs,
        out_specs=pl.BlockSpec((tm, tn), lambda gi, j, i: (tile(gi, i), j)),
        out_shape=jax.ShapeDtypeStruct((m_rows, n_tiles * tn), BF16),
        scratch_shapes=[pltpu.VMEM((group * tm, d), BF16)],
        compiler_params=_cparams(3),
        name=name,
    )(*args)


def _attn_kernel(*refs, layer, lc, ll, n_sub, heads, out_scale):
    if ll:
        q_ref, kc_ref, vc_ref, kl_ref, vl_ref, g_ref, lam_ref, sg_ref, out_ref, vt_ref, *slots = refs
        k_parts = [(kc_ref, 0, lc), (kl_ref, lc, ll)]
        v_parts = [(vc_ref, 0, lc), (vl_ref, lc, ll)]
    else:
        q_ref, kc_ref, vc_ref, g_ref, lam_ref, sg_ref, out_ref, vt_ref, *slots = refs
        k_parts = [(kc_ref, 0, lc)]
        v_parts = [(vc_ref, 0, lc)]
    lk = lc + ll
    st_ref = slots[0:2]
    pt_ref = slots[2:4]

    def head_lanes(hh):
        return slice(hh * LANES, (hh + 1) * LANES)

    for hh in range(heads):
        for ref, off, n in v_parts:
            for c in range(n // KV_CHUNK):
                blk = ref[c * KV_CHUNK:(c + 1) * KV_CHUNK, head_lanes(hh)].astype(F32)
                vt_ref[hh, :, off + c * KV_CHUNK:off + (c + 1) * KV_CHUNK] = blk.T.astype(BF16)

    row = lax.broadcasted_iota(jnp.int32, (ATT_VD, Q_SUB), 0)
    lam = lam_ref[0][0:1, :]
    sg = sg_ref[layer:layer + 1, :]

    def item(t):
        return t // n_sub, pl.ds((t % n_sub) * Q_SUB, Q_SUB)

    def scores(t, slot):
        hh, q_rows = item(t)
        qt = q_ref[q_rows, head_lanes(hh)].astype(F32).T
        qbd = jnp.concatenate([jnp.where(row < ATT_HD, qt, 0.0),
                               jnp.where(row >= ATT_HD, qt, 0.0)], axis=1).astype(BF16)
        m8 = None
        for k_ref, off, n in k_parts:
            for c in range(n // KV_CHUNK):
                kch = k_ref[c * KV_CHUNK:(c + 1) * KV_CHUNK, head_lanes(hh)]
                st = jnp.dot(kch, qbd, preferred_element_type=F32)
                st_ref[slot][off + c * KV_CHUNK:off + (c + 1) * KV_CHUNK, :] = st
                for r in range(KV_CHUNK // 8):
                    piece = st[r * 8:(r + 1) * 8, :]
                    m8 = piece if m8 is None else jnp.maximum(m8, piece)
        return jnp.max(m8, axis=0, keepdims=True)

    def softmax(slot, m):
        l8 = None
        marks = []
        m8 = jnp.broadcast_to(m, (8, 2 * Q_SUB))
        for g in range(lk // SM_GROUP):
            rows = slice(g * SM_GROUP, (g + 1) * SM_GROUP)
            m_use = m8 if g < SM_AHEAD else m8 + marks[g - SM_AHEAD] * 0.0
            p = jnp.exp2(st_ref[slot][rows, :] - jnp.tile(m_use, (SM_GROUP // 8, 1)))
            for r in range(SM_GROUP // 8):
                piece = p[r * 8:(r + 1) * 8, :]
                l8 = piece if l8 is None else l8 + piece
            marks.append(l8)
            pt_ref[slot][rows, :] = p.astype(BF16)
        return l8

    def pv(t, slot, l8):
        hh, q_rows = item(t)
        acc = jnp.dot(vt_ref[hh], pt_ref[slot][...], preferred_element_type=F32)
        o = acc * (1.0 / jnp.sum(l8, axis=0, keepdims=True))
        ot = o[:, :Q_SUB] - lam * o[:, Q_SUB:]
        oq = ot.T
        ms = jnp.mean(oq * oq, axis=-1, keepdims=True)
        y = oq * lax.rsqrt(ms + EPS) * sg * out_scale
        gate = g_ref[q_rows, head_lanes(hh)].astype(F32)
        out_ref[q_rows, head_lanes(hh)] = (y * _silu(gate)).astype(out_ref.dtype)

    def tick(t, slot, carry, do_scores=True, do_softmax=True, do_pv=True):
        m_prev, l8_prev = carry
        m = scores(t, slot) if do_scores else None
        l8 = softmax(1 - slot, m_prev) if do_softmax else None
        if do_pv:
            pv(t - 2, slot, l8_prev)
        return m, l8

    n_items = heads * n_sub
    carry = (None, None)
    for t in range(n_items + 2):
        carry = tick(t, t % 2, carry, do_scores=t < n_items, do_softmax=1 <= t <= n_items, do_pv=t >= 2)


def _attention(q_arr, kv_ctx, kv_lat, lam, subln_g, *, layer, batch, lq, heads, q_col, g_col, lc, kc_col, vc_col,
               ll, kl_col, vl_col, out_scale, name):
    width = heads * LANES
    assert ATT_HEADS % heads == 0 and all(c % heads == 0 for c in (q_col, g_col, kc_col, vc_col, kl_col, vl_col))
    col = lambda c: (lambda b, h: (b, c // heads + h))
    in_specs = [pl.BlockSpec((lq, width), col(q_col)),
                pl.BlockSpec((lc, width), col(kc_col)),
                pl.BlockSpec((lc, width), col(vc_col))]
    args = [q_arr, kv_ctx, kv_ctx]
    if kv_lat is not None:
        in_specs += [pl.BlockSpec((ll, width), col(kl_col)),
                     pl.BlockSpec((ll, width), col(vl_col))]
        args += [kv_lat, kv_lat]
    else:
        ll = 0
    in_specs += [pl.BlockSpec((lq, width), col(g_col)),
                 pl.BlockSpec((1, 8, LANES), lambda b, h: (layer, 0, 0)),
                 pl.BlockSpec(subln_g.shape, lambda b, h: (0, 0))]
    args += [q_arr, lam, subln_g]
    kern = functools.partial(_attn_kernel, layer=layer, lc=lc, ll=ll, n_sub=lq // Q_SUB, heads=heads, out_scale=out_scale)
    return pl.pallas_call(
        kern,
        grid=(batch, ATT_HEADS // heads),
        in_specs=in_specs,
        out_specs=pl.BlockSpec((lq, width), lambda b, h: (b, h)),
        out_shape=jax.ShapeDtypeStruct((batch * lq, ATT_HEADS * ATT_VD), BF16),
        scratch_shapes=[pltpu.VMEM((heads, ATT_VD, lc + ll), BF16),
                        pltpu.VMEM((lc + ll, 2 * Q_SUB), F32), pltpu.VMEM((lc + ll, 2 * Q_SUB), F32),
                        pltpu.VMEM((lc + ll, 2 * Q_SUB), BF16), pltpu.VMEM((lc + ll, 2 * Q_SUB), BF16)],
        compiler_params=_cparams(2),
        name=name,
    )(*args)


def _poolconv_kernel(up_c, up_p, up_n, gp_ref, a_c, a_p, a_n, b_c, b_p, b_n, gc_ref,
                     wpool_ref, pscale_ref, wdw_ref, bdw_ref, lng_ref, lnb_ref, wpw_ref,
                     ypool_ref, yconv_ref, u_ref, p_ref, z_ref, s_ref, r_ref, w8_ref, *, layer, rows, tiles_per_seq,
                     seq_len, row_blk):
    iseq = pl.program_id(0) % tiles_per_seq
    has_prev = iseq > 0
    has_next = iseq < tiles_per_seq - 1

    def glu(a_ref, b_ref):
        return a_ref[...].astype(F32) * jax.nn.sigmoid(b_ref[...].astype(F32))

    u_ref[0:HALO, :] = jnp.where(has_prev, glu(a_p, b_p), 0.0)
    u_ref[HALO:HALO + rows, :] = glu(a_c, b_c)
    u_ref[HALO + rows:, :] = jnp.where(has_next, glu(a_n, b_n), 0.0)
    p_ref[0:HALO, :] = jnp.where(has_prev, up_p[...].astype(F32), 0.0)
    p_ref[HALO:HALO + rows, :] = up_c[...].astype(F32)
    p_ref[HALO + rows:, :] = jnp.where(has_next, up_n[...].astype(F32), 0.0)

    ext = rows + 2 * HALO - 8
    for b in range(1, 8):
        s_ref[b - 1] = u_ref[b:b + ext, :]

    width = u_ref.shape[1]

    @pl.when(pl.program_id(0) == 0)
    def _():
        for j in range(CONV_K):
            w8_ref[8 * j:8 * j + 8, :] = jnp.broadcast_to(wdw_ref[j:j + 1, :], (8, width))

    lrow = slice(layer, layer + 1)
    for rb in range(rows // row_blk):
        acc = None
        for j in range(CONV_K):
            off = HALO - CONV_K // 2 + j
            base = rb * row_blk + (off // 8) * 8
            if off % 8 == 0:
                tap = u_ref[base:base + row_blk, :]
            else:
                tap = s_ref[off % 8 - 1, base:base + row_blk, :]
            term = (tap.reshape(row_blk // 8, 8, width) * w8_ref[8 * j:8 * j + 8, :]).reshape(row_blk, width)
            acc = term if acc is None else acc + term
        y = acc + bdw_ref[lrow, :]
        mu = jnp.mean(y, axis=-1, keepdims=True)
        yc = y - mu
        var = jnp.mean(yc * yc, axis=-1, keepdims=True)
        z = yc * lax.rsqrt(var + EPS) * lng_ref[lrow, :] + lnb_ref[lrow, :]
        z_ref[rb * row_blk:(rb + 1) * row_blk, :] = _silu(z).astype(BF16)
    yconv = jnp.dot(z_ref[...], wpw_ref[...].astype(BF16), preferred_element_type=F32)
    yconv_ref[...] = (yconv * _silu(gc_ref[...].astype(F32))).astype(yconv_ref.dtype)

    t = iseq * rows + lax.broadcasted_iota(jnp.int32, (rows, POOL_GD), 0)
    for g, w in enumerate(POOL_WINDOWS):
        cols = slice(g * POOL_GD, (g + 1) * POOL_GD)
        hw = w // 2
        load = lambda s, m, _c=cols: p_ref[s:s + m, _c]
        n, level = 1, 0
        while n < hw:
            dst = r_ref.at[level % 2]
            dst[8:rows + HALO + 8, :] = load(8, rows + HALO) + load(8 + n, rows + HALO)
            dst[rows + HALO + 8:, :] = jnp.zeros((8, POOL_GD), F32)
            load = lambda s, m, _d=dst: _d[s:s + m, :]
            n, level = 2 * n, level + 1
        ssum = load(HALO - hw, rows) + load(HALO, rows)
        cnt = (jnp.minimum(t + hw, seq_len) - jnp.maximum(t - hw, 0)).astype(F32)
        dlt = ssum / cnt - p_ref[HALO:HALO + rows, cols]
        yp = jnp.dot(dlt.astype(BF16), wpool_ref[g].astype(BF16), preferred_element_type=F32) * pscale_ref[lrow, cols]
        ypool_ref[:, cols] = (yp * _silu(gp_ref[:, cols].astype(F32))).astype(ypool_ref.dtype)


def _poolconv(proj, wpool, pscale, wdw, bdw, lng, lnb, wpw, *, layer, seq_len, rows, col0, width, name):
    m_rows = proj.shape[0]
    tiles_per_seq = seq_len // rows
    hpr = rows // HALO
    n_halo = m_rows // HALO
    cb = col0 // width

    def cur(k):
        return pl.BlockSpec((rows, width), lambda i: (i, cb + k))

    def prev(k):
        return pl.BlockSpec((HALO, width), lambda i: (jnp.maximum(i * hpr - 1, 0), cb + k))

    def nxt(k):
        return pl.BlockSpec((HALO, width), lambda i: (jnp.minimum((i + 1) * hpr, n_halo - 1), cb + k))

    full = lambda a: (pl.BlockSpec(a.shape, lambda i: (0, 0)) if a.ndim == 2 else
                      pl.BlockSpec((None,) + a.shape[1:], lambda i: (layer,) + (0,) * (a.ndim - 1)))
    consts = [wpool, pscale, wdw, bdw, lng, lnb, wpw]
    kern = functools.partial(_poolconv_kernel, layer=layer, rows=rows, tiles_per_seq=tiles_per_seq, seq_len=seq_len,
                             row_blk=min(32, rows))
    return pl.pallas_call(
        kern,
        grid=(m_rows // rows,),
        in_specs=[cur(0), prev(0), nxt(0), cur(1), cur(2), prev(2), nxt(2), cur(3), prev(3), nxt(3), cur(4)]
                 + [full(a) for a in consts],
        out_specs=[pl.BlockSpec((rows, width), lambda i: (i, 0)),
                   pl.BlockSpec((rows, width), lambda i: (i, 0))],
        out_shape=[jax.ShapeDtypeStruct((m_rows, width), BF16),
                   jax.ShapeDtypeStruct((m_rows, width), BF16)],
        scratch_shapes=[pltpu.VMEM((rows + 2 * HALO, width), F32),
                        pltpu.VMEM((rows + 2 * HALO, width), F32),
                        pltpu.VMEM((rows, width), BF16),
                        pltpu.VMEM((7, rows + 2 * HALO - 8, width), F32),
                        pltpu.VMEM((2, rows + 2 * HALO, POOL_GD), F32),
                        pltpu.VMEM((8 * CONV_K, width), F32)],
        compiler_params=_cparams(1),
        name=name,
    )(*([proj] * 11), *consts)


def _outproj_kernel(*refs, d_model, att_width, pool_width, final):
    if final:
        ya_ref, yp_ref, yc_ref, w_ref, x_ref, mod_ref, fg_ref, out_ref = refs
    else:
        ya_ref, yp_ref, yc_ref, w_ref, x_ref, mod_ref, out_ref = refs
    c1 = att_width
    c2 = att_width + pool_width
    y = jnp.dot(ya_ref[...], w_ref[0:c1, :].astype(BF16), preferred_element_type=F32)
    y = y + jnp.dot(yp_ref[...], w_ref[c1:c2, :].astype(BF16), preferred_element_type=F32)
    y = y + jnp.dot(yc_ref[...], w_ref[c2:, :].astype(BF16), preferred_element_type=F32)
    if final:
        gate = mod_ref[0][:, 2 * d_model:]
    else:
        tn = out_ref.shape[1]
        gate = mod_ref[0, :, pl.ds(pl.multiple_of(2 * d_model + pl.program_id(0) * tn, tn), tn)]
    xn = x_ref[...] + gate * y
    if final:
        ms = jnp.mean(xn * xn, axis=-1, keepdims=True)
        xn = xn * lax.rsqrt(ms + EPS) * fg_ref[...]
    out_ref[...] = xn


def _outproj(ya, yp, yc, w, x2d, mod3, final_g, *, layer, tm, tiles_per_mod, mod_row0, name):
    m_rows, d = x2d.shape
    final = final_g is not None
    if not final:
        tn = d // 2
        rows_per_mod = tm * tiles_per_mod
        tm = min(2 * tm, rows_per_mod, m_rows)
        tiles_per_mod = rows_per_mod // tm
        in_specs = [pl.BlockSpec((tm, ya.shape[1]), lambda j, i: (i, 0)),
                    pl.BlockSpec((tm, yp.shape[1]), lambda j, i: (i, 0)),
                    pl.BlockSpec((tm, yc.shape[1]), lambda j, i: (i, 0)),
                    pl.BlockSpec((None, d, tn), lambda j, i: (layer, 0, j)),
                    pl.BlockSpec((tm, tn), lambda j, i: (i, j)),
                    pl.BlockSpec((1, 1, mod3.shape[-1]),
                                 lambda j, i: (8 * layer + mod_row0 + i // tiles_per_mod, 0, 0))]
        kern = functools.partial(_outproj_kernel, d_model=d, att_width=ya.shape[1], pool_width=yp.shape[1],
                                 final=False)
        return pl.pallas_call(
            kern,
            grid=(d // tn, m_rows // tm),
            in_specs=in_specs,
            out_specs=pl.BlockSpec((tm, tn), lambda j, i: (i, j)),
            out_shape=jax.ShapeDtypeStruct((m_rows, d), F32),
            compiler_params=_cparams(2),
            name=name,
        )(ya, yp, yc, w, x2d, mod3)
    in_specs = [pl.BlockSpec((tm, ya.shape[1]), lambda i: (i, 0)),
                pl.BlockSpec((tm, yp.shape[1]), lambda i: (i, 0)),
                pl.BlockSpec((tm, yc.shape[1]), lambda i: (i, 0)),
                pl.BlockSpec((None,) + w.shape[1:], lambda i: (layer, 0, 0), pipeline_mode=pl.Buffered(1)),
                pl.BlockSpec((tm, d), lambda i: (i, 0)),
                pl.BlockSpec((1, 1, mod3.shape[-1]), lambda i: (8 * layer + mod_row0 + i // tiles_per_mod, 0, 0))]
    args = [ya, yp, yc, w, x2d, mod3]
    if final:
        in_specs.append(pl.BlockSpec((1, d), lambda i: (0, 0)))
        args.append(final_g)
    kern = functools.partial(_outproj_kernel, d_model=d, att_width=ya.shape[1], pool_width=yp.shape[1],
                             final=final)
    return pl.pallas_call(
        kern,
        grid=(m_rows // tm,),
        in_specs=in_specs,
        out_specs=pl.BlockSpec((tm, d), lambda i: (i, 0)),
        out_shape=jax.ShapeDtypeStruct((m_rows, d), F32),
        compiler_params=_cparams(1),
        name=name,
    )(*args)


def _rope_tables(seq_len):
    n_freq = ATT_HD // 4
    inv_freq = (np.float32(ROPE_BASE) ** (-np.arange(n_freq, dtype=np.float32) / n_freq)).astype(np.float32)
    t = np.arange(seq_len)
    lane = np.arange(LANES)
    in_col_half = (lane % ATT_HD) >= ATT_HD // 2
    pos = np.where(in_col_half[None, :], (t % GRID_W)[:, None], (t // GRID_W)[:, None]).astype(np.float32)
    ang = pos * inv_freq[lane % n_freq][None, :]
    cos = np.cos(ang).astype(np.float32)
    sin = np.sin(ang).astype(np.float32)
    qs = np.float32(ATT_HD ** -0.5 * math.log2(math.e))
    perm = np.zeros((2 * LANES, 2 * LANES), np.float32)
    for p in range(2 * LANES):
        if p % (2 * n_freq) < n_freq:
            perm[p + n_freq, p] = -1.0
        else:
            perm[p - n_freq, p] = 1.0
    return (jnp.asarray(np.stack([cos * qs, cos])), jnp.asarray(np.stack([sin * qs, sin])),
            jnp.asarray(perm, dtype=BF16))


def kernel(x, c, ctx, c_ctx, w_mod, b_mod, norm_g, w_in, lambda_q1, lambda_k1, lambda_q2, lambda_k2,
           subln_g, w_pool, pool_scale, w_dw, b_dw, conv_ln_g, conv_ln_b, w_pw2, w_out, final_g):
    batch, seq, d = x.shape
    lc = ctx.shape[1]
    depth = w_mod.shape[0]
    att_w = d // 2
    pool_w = d // 4
    assert batch < 8 and seq % GRID_W == 0 and att_w == ATT_HEADS * ATT_VD and pool_w == 4 * POOL_GD
    assert seq % KV_CHUNK == 0 and lc % KV_CHUNK == 0

    cc = jnp.concatenate([c, c_ctx[None, :], jnp.zeros((8 - batch - 1, d), F32)], axis=0)
    lam_init = [0.8 - 0.6 * math.exp(-0.3 * l) for l in range(depth)]
    li = jnp.broadcast_to(jnp.asarray(lam_init, F32)[:, None, None], (depth, 1, LANES))
    mod, lam = _modulation(cc, w_mod, b_mod, lambda_q1, lambda_k1, lambda_q2, lambda_k2, li)
    tabs = _rope_tables(seq)

    tm_lat = min(1024, seq)
    lat_group = math.gcd(batch * seq // tm_lat, 4)
    tn = 512
    n_in = w_in.shape[-1]
    hcol = att_w // LANES
    pc_col0 = 4 * att_w
    x2d = x.reshape(batch * seq, d)
    c2d = ctx.reshape(batch * lc, d)
    row = lambda a: a.reshape(1, -1)
    mod3 = mod.reshape(depth * 8, 1, 3 * d)
    lw = (w_pool, pool_scale, w_dw, b_dw, conv_ln_g, conv_ln_b, w_pw2)
    g_l = norm_g
    sg_l = subln_g

    for l in range(depth):
        last = l == depth - 1
        out_scale = 1.0 - lam_init[l]

        proj = _inproj(x2d, mod3, g_l, w_in, tabs, layer=l, tm=tm_lat, tn=tn, group=lat_group, tiles_per_mod=seq // tm_lat,
                       mod_row0=0, n0=0, n_tiles=n_in // tn, att_width=att_w, name=f"inproj_lat{l}")
        if last:
            cproj = _inproj(c2d, mod3, g_l, w_in, None, layer=l, tm=batch * lc, tn=tn, group=1, tiles_per_mod=1,
                            mod_row0=batch, n0=att_w // tn, n_tiles=2 * att_w // tn, att_width=att_w,
                            name=f"inproj_ctx{l}")
            kc_col, vc_col = 0, hcol
        else:
            cproj = _inproj(c2d, mod3, g_l, w_in, None, layer=l, tm=batch * lc, tn=tn, group=1, tiles_per_mod=1,
                            mod_row0=batch, n0=0, n_tiles=n_in // tn, att_width=att_w,
                            name=f"inproj_ctx{l}")
            kc_col, vc_col = hcol, 2 * hcol

        y_att = _attention(proj, cproj, proj, lam, sg_l, layer=l, batch=batch, lq=seq,
                           heads=LAT_HEADS_PER_STEP, q_col=0, g_col=3 * hcol, lc=lc, kc_col=kc_col, vc_col=vc_col,
                           ll=seq, kl_col=hcol, vl_col=2 * hcol, out_scale=out_scale, name=f"attn_lat{l}")
        y_pool, y_conv = _poolconv(proj, *lw, layer=l, seq_len=seq, rows=min(1024, seq), col0=pc_col0, width=pool_w,
                                   name=f"poolconv_lat{l}")
        if not last:
            yc_att = _attention(cproj, cproj, None, lam, sg_l, layer=l, batch=batch, lq=lc,
                                heads=ATT_HEADS, q_col=0, g_col=3 * hcol, lc=lc, kc_col=kc_col, vc_col=vc_col,
                                ll=0, kl_col=0, vl_col=0, out_scale=out_scale, name=f"attn_ctx{l}")
            yc_pool, yc_conv = _poolconv(cproj, *lw, layer=l, seq_len=lc, rows=min(256, lc), col0=pc_col0,
                                         width=pool_w, name=f"poolconv_ctx{l}")
            c2d = _outproj(yc_att, yc_pool, yc_conv, w_out, c2d, mod3, None, layer=l, tm=min(512, batch * lc),
                           tiles_per_mod=batch * lc, mod_row0=batch, name=f"outproj_ctx{l}")
        tm_o = min(512, seq)
        x2d = _outproj(y_att, y_pool, y_conv, w_out, x2d, mod3, row(final_g) if last else None,
                       layer=l, tm=tm_o, tiles_per_mod=seq // tm_o, mod_row0=0, name=f"outproj_lat{l}")
    return x2d.reshape(batch, seq, d)
```
